```python
import math
import jax, jax.numpy as jnp
from jax import lax
import numpy as np

D_MODEL = 1024
BATCH = 32
SEQ = 2048
DEPTH = 1
DEC_BATCH = 128
DEC_SEQ = 1
PAST_LEN = 8192
PAGE_SIZE = 128

REC_WIDTH = D_MODEL // 2
REC_HEAD_DIM = 128
REC_HEADS = REC_WIDTH // REC_HEAD_DIM
ATT_WIDTH = D_MODEL // 2
ATT_HEAD_DIM = 64
ATT_V_DIM = 2 * ATT_HEAD_DIM
ATT_HEADS = ATT_WIDTH // ATT_V_DIM
N_EXPERTS = 32
TOP_K = 4
D_FF_EXPERT = D_MODEL
SWIGLU_LIMIT = 7.0
SWIGLU_ALPHA = 1.702
CHUNK = 64
Q_BLOCK = 128
EPS = 1e-6
IN_SPLITS = [int(v) for v in np.cumsum([REC_WIDTH, REC_WIDTH, REC_WIDTH, REC_WIDTH, ATT_WIDTH, ATT_WIDTH, ATT_WIDTH])]
N_IN = 4 * REC_WIDTH + 3 * ATT_WIDTH + 2 * D_MODEL

kernel_name = "hgrn2_diffattn_gated_moe_step"


def lambda_init(layer):
    return 0.8 - 0.6 * math.exp(-0.3 * layer)


def rmsnorm(x, w):
    xf = x.astype(jnp.float32)
    y = xf * lax.rsqrt(jnp.mean(xf * xf, axis=-1, keepdims=True) + EPS)
    return (y * w).astype(x.dtype)


def project_inputs(x, ln_w, w_in, lb, q_norm_w, k_norm_w):
    B, L, _ = x.shape
    xn = rmsnorm(x, ln_w)
    p = xn @ w_in
    rq, rf, ri, rg, aq, ak, av, gates = jnp.split(p, IN_SPLITS, axis=-1)
    shp = (B, L, REC_HEADS, REC_HEAD_DIM)
    zf = rf.reshape(shp).astype(jnp.float32)
    lbh = lb.reshape(REC_HEADS, REC_HEAD_DIM)
    log_f = jnp.log(lbh + (1.0 - lbh) * jax.nn.sigmoid(zf))
    rec_k = (1.0 - lbh) * jax.nn.sigmoid(-zf)
    rec = (rq.reshape(shp), rec_k, ri.reshape(shp), log_f, rg.reshape(shp))
    att_q = rmsnorm(aq.reshape(B, L, ATT_HEADS, 2, ATT_HEAD_DIM), q_norm_w)
    att_k = rmsnorm(ak.reshape(B, L, ATT_HEADS, 2, ATT_HEAD_DIM), k_norm_w)
    att_v = av.reshape(B, L, ATT_HEADS, ATT_V_DIM)
    g_rec, g_att = jnp.split(jax.nn.sigmoid(gates), 2, axis=-1)
    return rec, (att_q, att_k, att_v), g_rec, g_att


def gla_chunked(q, k, v, log_f, s0):
    B, L, H, DK = q.shape
    C = CHUNK if L >= CHUNK else L
    pad = (-L) % C
    q, k, v = (t.astype(jnp.float32) for t in (q, k, v))
    if pad:
        pw = ((0, 0), (0, pad), (0, 0), (0, 0))
        q, k, v, log_f = (jnp.pad(t, pw) for t in (q, k, v, log_f))
    Ln = L + pad
    N = Ln // C
    to_chunks = lambda t: jnp.transpose(t.reshape(B, N, C, H, t.shape[-1]), (1, 0, 3, 2, 4))
    q, k, v, log_f = (to_chunks(t) for t in (q, k, v, log_f))
    b = jnp.cumsum(log_f, axis=3)
    b_last = b[:, :, :, -1:, :]
    q_e = q * jnp.exp(b)
    k_e = k * jnp.exp(-b)
    k_tail = k * jnp.exp(b_last - b)
    decay_last = jnp.exp(b_last[:, :, :, 0, :])
    causal = jnp.tril(jnp.ones((C, C), dtype=bool))
    scores = jnp.where(causal, jnp.einsum('nbhid,nbhjd->nbhij', q_e, k_e), 0.0)
    o_intra = jnp.einsum('nbhij,nbhjv->nbhiv', scores, v)

    def step(S, xs):
        qe, kt, vv, dl = xs
        o = jnp.einsum('bhid,bhdv->bhiv', qe, S)
        S = dl[..., None] * S + jnp.einsum('bhjd,bhjv->bhdv', kt, vv)
        return S, o

    s_final, o_inter = lax.scan(step, s0.astype(jnp.float32), (q_e, k_tail, v, decay_last))
    o = jnp.transpose(o_intra + o_inter, (1, 0, 3, 2, 4)).reshape(B, Ln, H, v.shape[-1])[:, :L]
    return o, s_final


def recurrent_branch(rec, s0, norm_w, w_up):
    q, k, v, log_f, g = rec
    B, L = q.shape[:2]
    o, s_final = gla_chunked(q, k, v, log_f, s0)
    o = rmsnorm(o.astype(q.dtype), norm_w) * jax.nn.silu(g)
    return o.reshape(B, L, REC_WIDTH) @ w_up, s_final


def diff_attend(q, k, v, valid, lam):
    s = jnp.einsum('bqhcd,bkhcd->bhcqk', q, k).astype(jnp.float32) * (ATT_HEAD_DIM ** -0.5)
    s = jnp.where(valid, s, jnp.finfo(jnp.float32).min)
    p = jax.nn.softmax(s, axis=-1)
    p = p[:, :, 0] - lam * p[:, :, 1]
    return jnp.einsum('bhqk,bkhe->bqhe', p.astype(v.dtype), v)


def prompt_attention(q, k, v, lam):
    B, L = q.shape[:2]
    qb = Q_BLOCK if L % Q_BLOCK == 0 else L
    nb = L // qb
    q_blocks = jnp.moveaxis(q.reshape(B, nb, qb, ATT_HEADS, 2, ATT_HEAD_DIM), 1, 0)
    kpos = jnp.arange(L)

    def one_block(args):
        q_blk, i = args
        qpos = i * qb + jnp.arange(qb)
        return diff_attend(q_blk, k, v, kpos[None, :] <= qpos[:, None], lam)

    o = lax.map(one_block, (q_blocks, jnp.arange(nb)))
    return jnp.moveaxis(o, 0, 1).reshape(B, L, ATT_HEADS, ATT_V_DIM)


def sample_attention(q, k_new, v_new, cache_k_l, cache_v_l, page_table, lam):
    Bd, Sn = q.shape[:2]
    k_past = cache_k_l[page_table].reshape(Bd, PAST_LEN, ATT_HEADS, 2, ATT_HEAD_DIM)
    v_past = cache_v_l[page_table].reshape(Bd, PAST_LEN, ATT_HEADS, ATT_V_DIM)
    k_all = jnp.concatenate([k_past, k_new.astype(k_past.dtype)], axis=1)
    v_all = jnp.concatenate([v_past, v_new.astype(v_past.dtype)], axis=1)
    kpos = jnp.arange(PAST_LEN + Sn)
    qpos = PAST_LEN + jnp.arange(Sn)
    return diff_attend(q, k_all, v_all, kpos[None, :] <= qpos[:, None], lam)


def attention_output(o, subln_w, lam_init, w_up):
    B, L = o.shape[:2]
    o = rmsnorm(o, subln_w) * (1.0 - lam_init)
    return o.reshape(B, L, ATT_WIDTH) @ w_up


def moe_ffn(x, w_router, b_router, w_up, b_up, w_down, b_down):
    B, L, D = x.shape
    t = x.reshape(B * L, D)
    logits = (t @ w_router).astype(jnp.float32) + b_router.astype(jnp.float32)
    top_v, top_i = lax.top_k(logits, TOP_K)
    wts = jax.nn.softmax(top_v, axis=-1)
    gate = jnp.sum(jax.nn.one_hot(top_i, N_EXPERTS, dtype=jnp.float32) * wts[..., None], axis=1)
    y = jnp.zeros((B * L, D), jnp.float32)
    for e in range(N_EXPERTS):
        h = t @ w_up[e] + b_up[e]
        glu = jnp.minimum(h[:, 0::2], SWIGLU_LIMIT)
        lin = jnp.clip(h[:, 1::2], -SWIGLU_LIMIT, SWIGLU_LIMIT)
        a = glu * jax.nn.sigmoid(SWIGLU_ALPHA * glu) * (lin + 1.0)
        y = y + gate[:, e:e + 1] * (a @ w_down[e] + b_down[e]).astype(jnp.float32)
    return y.astype(x.dtype).reshape(B, L, D)


def setup_inputs(seed: int = 0) -> dict:
    key = jax.random.key(seed)
    ks = jax.random.split(key, 32)
    f32 = jnp.float32
    nrm = lambda k, shape, scale: scale * jax.random.normal(k, shape, f32)
    gain = lambda k, shape: 1.0 + 0.05 * jax.random.normal(k, shape, f32)
    n_pages = PAST_LEN // PAGE_SIZE
    n_used = DEC_BATCH * n_pages
    n_pool = n_used + n_used // 4
    page_table = jax.random.permutation(ks[5], n_pool)[:n_used].reshape(DEC_BATCH, n_pages).astype(jnp.int32)
    return {
        "x_prompt": nrm(ks[0], (BATCH, SEQ, D_MODEL), 1.0),
        "x_sample": nrm(ks[1], (DEC_BATCH, DEC_SEQ, D_MODEL), 1.0),
        "cache_k": nrm(ks[2], (DEPTH, n_pool, PAGE_SIZE, ATT_HEADS, 2, ATT_HEAD_DIM), 1.0),
        "cache_v": nrm(ks[3], (DEPTH, n_pool, PAGE_SIZE, ATT_HEADS, ATT_V_DIM), 1.0),
        "state_rec": nrm(ks[4], (DEPTH, DEC_BATCH, REC_HEADS, REC_HEAD_DIM, REC_HEAD_DIM), 0.5),
        "page_table": page_table,
        "ln1_w": gain(ks[6], (DEPTH, D_MODEL)),
        "w_in": nrm(ks[7], (DEPTH, D_MODEL, N_IN), D_MODEL ** -0.5),
        "rec_lb_logits": nrm(ks[8], (DEPTH + 1, REC_WIDTH), 0.1),
        "rec_norm_w": gain(ks[9], (DEPTH, REC_HEAD_DIM)),
        "w_up_rec": nrm(ks[10], (DEPTH, REC_WIDTH, D_MODEL), REC_WIDTH ** -0.5),
        "q_norm_w": gain(ks[11], (DEPTH, ATT_HEAD_DIM)),
        "k_norm_w": gain(ks[12], (DEPTH, ATT_HEAD_DIM)),
        "lambda_q1": nrm(ks[13], (DEPTH, ATT_HEAD_DIM), 0.1),
        "lambda_k1": nrm(ks[14], (DEPTH, ATT_HEAD_DIM), 0.1),
        "lambda_q2": nrm(ks[15], (DEPTH, ATT_HEAD_DIM), 0.1),
        "lambda_k2": nrm(ks[16], (DEPTH, ATT_HEAD_DIM), 0.1),
        "att_subln_w": gain(ks[17], (DEPTH, ATT_V_DIM)),
        "w_up_att": nrm(ks[18], (DEPTH, ATT_WIDTH, D_MODEL), ATT_WIDTH ** -0.5),
        "w_out": nrm(ks[19], (DEPTH, D_MODEL, D_MODEL), D_MODEL ** -0.5),
        "ln2_w": gain(ks[20], (DEPTH, D_MODEL)),
        "w_router": nrm(ks[21], (DEPTH, D_MODEL, N_EXPERTS), D_MODEL ** -0.5),
        "b_router": nrm(ks[22], (DEPTH, N_EXPERTS), 0.01),
        "w_exp_up": nrm(ks[23], (DEPTH, N_EXPERTS, D_MODEL, 2 * D_FF_EXPERT), D_MODEL ** -0.5),
        "b_exp_up": nrm(ks[24], (DEPTH, N_EXPERTS, 2 * D_FF_EXPERT), 0.02),
        "w_exp_down": nrm(ks[25], (DEPTH, N_EXPERTS, D_FF_EXPERT, D_MODEL), D_FF_EXPERT ** -0.5),
        "b_exp_down": nrm(ks[26], (DEPTH, N_EXPERTS, D_MODEL), 0.02),
    }


def reference(x_prompt, x_sample, cache_k, cache_v, state_rec, page_table, ln1_w, w_in, rec_lb_logits,
              rec_norm_w, w_up_rec, q_norm_w, k_norm_w, lambda_q1, lambda_k1, lambda_q2, lambda_k2,
              att_subln_w, w_up_att, w_out, ln2_w, w_router, b_router, w_exp_up, b_exp_up,
              w_exp_down, b_exp_down):
    f32 = jnp.float32
    lower_bounds = jnp.cumsum(jax.nn.softmax(rec_lb_logits.astype(f32), axis=0), axis=0)
    xp, xs = x_prompt, x_sample
    kp_l, vp_l, sp_l, ks_l, vs_l, ss_l = [], [], [], [], [], []
    for l in range(DEPTH):
        lam_init = lambda_init(l)
        lam = (jnp.exp(jnp.sum(lambda_q1[l].astype(f32) * lambda_k1[l].astype(f32)))
               - jnp.exp(jnp.sum(lambda_q2[l].astype(f32) * lambda_k2[l].astype(f32))) + lam_init)
        lb = lower_bounds[l]

        rec, (q, k, v), g_rec, g_att = project_inputs(xp, ln1_w[l], w_in[l], lb, q_norm_w[l], k_norm_w[l])
        s0 = jnp.zeros((xp.shape[0], REC_HEADS, REC_HEAD_DIM, REC_HEAD_DIM), f32)
        y_rec, s_p = recurrent_branch(rec, s0, rec_norm_w[l], w_up_rec[l])
        y_att = attention_output(prompt_attention(q, k, v, lam), att_subln_w[l], lam_init, w_up_att[l])
        hp = xp + (g_rec * y_rec + g_att * y_att) @ w_out[l]
        xp = hp + moe_ffn(rmsnorm(hp, ln2_w[l]), w_router[l], b_router[l], w_exp_up[l], b_exp_up[l],
                          w_exp_down[l], b_exp_down[l])
        kp_l.append(k); vp_l.append(v); sp_l.append(s_p.astype(state_rec.dtype))

        rec, (q, k, v), g_rec, g_att = project_inputs(xs, ln1_w[l], w_in[l], lb, q_norm_w[l], k_norm_w[l])
        y_rec, s_s = recurrent_branch(rec, state_rec[l], rec_norm_w[l], w_up_rec[l])
        o = sample_attention(q, k, v, cache_k[l], cache_v[l], page_table, lam)
        y_att = attention_output(o, att_subln_w[l], lam_init, w_up_att[l])
        hs = xs + (g_rec * y_rec + g_att * y_att) @ w_out[l]
        xs = hs + moe_ffn(rmsnorm(hs, ln2_w[l]), w_router[l], b_router[l], w_exp_up[l], b_exp_up[l],
                          w_exp_down[l], b_exp_down[l])
        ks_l.append(k); vs_l.append(v); ss_l.append(s_s.astype(state_rec.dtype))

    new_k_prompt = jnp.stack(kp_l)
    new_v_prompt = jnp.stack(vp_l)
    new_rec_prompt = jnp.stack(sp_l)
    new_k_sample = jnp.stack(ks_l)
    new_v_sample = jnp.stack(vs_l)
    new_rec_sample = jnp.stack(ss_l)
    return (xp, xs, new_k_prompt, new_v_prompt, new_rec_prompt, new_k_sample, new_v_sample, new_rec_sample)
```

```python
import functools
import math

import numpy as np
import jax
import jax.numpy as jnp
from jax import lax
from jax.experimental import pallas as pl
from jax.experimental.pallas import tpu as pltpu

F32 = jnp.float32
BF16 = jnp.bfloat16

D_MODEL = 1024
REC_WIDTH = 512
REC_HEAD_DIM = 128
REC_HEADS = 4
ATT_WIDTH = 512
ATT_HEAD_DIM = 64
ATT_V_DIM = 128
ATT_HEADS = 4
N_EXPERTS = 32
TOP_K = 4
D_FF = 1024
SWIGLU_LIMIT = 7.0
SWIGLU_ALPHA = 1.702
CHUNK = 64
EPS = 1e-6
N_IN = 4 * REC_WIDTH + 3 * ATT_WIDTH + 2 * D_MODEL
LANES = 128
VMEM_LIMIT = 48 * 1024 * 1024

TOKEN_TILE = 512
GLA_BLOCK = 256
ATT_TILE = 512
MOE_TILE = 512


def _sigmoid(x):
    return 1.0 / (1.0 + jnp.exp(-x))


def _dot(a, b):
    return jnp.dot(a, b, preferred_element_type=F32)


def _dot_nt(a, b):
    return lax.dot_general(a, b, (((1,), (1,)), ((), ())), preferred_element_type=F32)


def _dot_tn(a, b):
    return lax.dot_general(a, b, (((0,), (0,)), ((), ())), preferred_element_type=F32)


def _split_dot(x, m_bf16, terms):
    acc = None
    r = x
    for t in range(terms):
        p = r.astype(BF16)
        d = _dot(p, m_bf16)
        acc = d if acc is None else acc + d
        if t + 1 < terms:
            r = r - p.astype(F32)
    return acc


def _lambda_init(layer):
    return 0.8 - 0.6 * math.exp(-0.3 * layer)


def _lam_from_params(lp, lam_init):
    a = jnp.sum(lp[0:1, :] * lp[1:2, :], axis=-1, keepdims=True)
    b = jnp.sum(lp[2:3, :] * lp[3:4, :], axis=-1, keepdims=True)
    return jnp.exp(a) - jnp.exp(b) + lam_init


def _lower_bound(lbl):
    a0 = lbl[0:1, :]
    a1 = lbl[1:2, :]
    mx = jnp.maximum(a0, a1)
    e0 = jnp.exp(a0 - mx)
    e1 = jnp.exp(a1 - mx)
    return e0 / (e0 + e1)


def _in_proj_kernel(x_ref, ln_ref, w_ref, seg_ref, qn_ref, kn_ref,
                    rq_ref, zf_ref, rv_ref, rg_ref, aq_ref, ak_ref, av_ref, gate_ref):
    x = x_ref[...]
    ms = jnp.mean(x * x, axis=-1, keepdims=True)
    xn = (x * lax.rsqrt(ms + EPS) * ln_ref[...]).astype(BF16)

    def proj(lo, hi):
        return _dot(xn, w_ref[:, lo:hi])

    def seg_norm(a, w):
        ms_seg = _split_dot(a * a, seg_ref[...], 2)
        return a * lax.rsqrt(ms_seg + EPS) * w

    r = REC_WIDTH
    rq_ref[...] = proj(0, r).astype(BF16)
    zf_ref[...] = proj(r, 2 * r)
    rv_ref[...] = proj(2 * r, 3 * r).astype(BF16)
    rg_ref[...] = proj(3 * r, 4 * r).astype(BF16)
    a0 = 4 * r
    aq = seg_norm(proj(a0, a0 + ATT_WIDTH), qn_ref[...])
    aq_ref[...] = (aq * (ATT_HEAD_DIM ** -0.5)).astype(BF16)
    ak_ref[...] = seg_norm(proj(a0 + ATT_WIDTH, a0 + 2 * ATT_WIDTH), kn_ref[...])
    av_ref[...] = proj(a0 + 2 * ATT_WIDTH, a0 + 3 * ATT_WIDTH)
    g0 = a0 + 3 * ATT_WIDTH
    gate_ref[...] = _sigmoid(proj(g0, g0 + 2 * D_MODEL)).astype(BF16)


def _in_proj(x2d, ln_w, w_in_bf16, seg_mat, qn_t, kn_t):
    t = x2d.shape[0]
    tm = min(TOKEN_TILE, t)
    assert t % tm == 0
    row = lambda w: pl.BlockSpec((tm, w), lambda i: (i, 0))
    full = lambda a: pl.BlockSpec(a.shape, lambda i: (0,) * a.ndim)
    out_shapes = (
        jax.ShapeDtypeStruct((t, REC_WIDTH), BF16),
        jax.ShapeDtypeStruct((t, REC_WIDTH), F32),
        jax.ShapeDtypeStruct((t, REC_WIDTH), BF16),
        jax.ShapeDtypeStruct((t, REC_WIDTH), BF16),
        jax.ShapeDtypeStruct((t, ATT_WIDTH), BF16),
        jax.ShapeDtypeStruct((t, ATT_WIDTH), F32),
        jax.ShapeDtypeStruct((t, ATT_WIDTH), F32),
        jax.ShapeDtypeStruct((t, 2 * D_MODEL), BF16),
    )
    return pl.pallas_call(
        _in_proj_kernel,
        grid=(t // tm,),
        in_specs=[row(D_MODEL), full(ln_w), full(w_in_bf16), full(seg_mat), full(qn_t), full(kn_t)],
        out_specs=tuple(row(s.shape[1]) for s in out_shapes),
        out_shape=out_shapes,
        compiler_params=pltpu.CompilerParams(dimension_semantics=("arbitrary",),
                                             vmem_limit_bytes=VMEM_LIMIT),
        name="in_proj",
    )(x2d, ln_w, w_in_bf16, seg_mat, qn_t, kn_t)


def _rec_out(o, nw, g):
    ms = jnp.mean(o * o, axis=-1, keepdims=True)
    return o * lax.rsqrt(ms + EPS) * nw * (g * _sigmoid(g))


def _gla_kernel(q_ref, zf_ref, v_ref, g_ref, lbl_ref, nw_ref, o_ref, sfin_ref, st_ref):
    j = pl.program_id(1)

    @pl.when(j == 0)
    def _():
        st_ref[...] = jnp.zeros_like(st_ref)

    lb = _lower_bound(lbl_ref[...])
    c = CHUNK
    ri = lax.broadcasted_iota(jnp.int32, (c, c), 0)
    ci = lax.broadcasted_iota(jnp.int32, (c, c), 1)
    causal = ri >= ci
    tri = jnp.where(causal, 1.0, 0.0).astype(BF16)
    nw = nw_ref[...]
    n_chunks = q_ref.shape[0] // c
    for n in range(n_chunks):
        rows = slice(n * c, (n + 1) * c)
        zf = zf_ref[rows, :]
        log_f = jnp.log(lb + (1.0 - lb) * _sigmoid(zf))
        k = (1.0 - lb) * _sigmoid(-zf)
        b = _split_dot_left(tri, log_f)
        b_last = b[c - 1:c, :]
        e_pos = jnp.exp(b)
        e_neg = jnp.exp(-b)
        e_tail = jnp.exp(b_last - b)
        decay = jnp.exp(b_last)
        q_e = (q_ref[rows, :].astype(F32) * e_pos).astype(BF16)
        k_e = (k * e_neg).astype(BF16)
        k_t = (k * e_tail).astype(BF16)
        v = v_ref[rows, :]
        g = g_ref[rows, :].astype(F32)
        for h in range(REC_HEADS):
            ln = slice(h * REC_HEAD_DIM, (h + 1) * REC_HEAD_DIM)
            s_t = st_ref[h]
            sc = jnp.where(causal, _dot_nt(q_e[:, ln], k_e[:, ln]), 0.0)
            o = _dot(sc.astype(BF16), v[:, ln]) + _dot_nt(q_e[:, ln], s_t.astype(BF16))
            st_ref[h] = decay[:, ln] * s_t + _dot_tn(v[:, ln], k_t[:, ln])
            o_ref[rows, ln] = _rec_out(o, nw[:, ln], g[:, ln]).astype(o_ref.dtype)

    @pl.when(j == pl.num_programs(1) - 1)
    def _():
        for h in range(REC_HEADS):
            sfin_ref[h] = st_ref[h].T


def _split_dot_left(m_bf16, x):
    p1 = x.astype(BF16)
    r1 = x - p1.astype(F32)
    p2 = r1.astype(BF16)
    p3 = (r1 - p2.astype(F32)).astype(BF16)
    return _dot(m_bf16, p1) + _dot(m_bf16, p2) + _dot(m_bf16, p3)


def _gla_prompt(rq, zf, rv, rg, lb_logits, nw_t):
    b, l, w = rq.shape
    assert l % CHUNK == 0
    lb_rows = min(GLA_BLOCK, l)
    assert l % lb_rows == 0
    seq = pl.BlockSpec((None, lb_rows, w), lambda i, j: (i, j, 0))
    full = lambda a: pl.BlockSpec(a.shape, lambda i, j: (0,) * a.ndim)
    return pl.pallas_call(
        _gla_kernel,
        grid=(b, l // lb_rows),
        in_specs=[seq, seq, seq, seq, full(lb_logits), full(nw_t)],
        out_specs=(seq, pl.BlockSpec((None, REC_HEADS, REC_HEAD_DIM, REC_HEAD_DIM),
                                     lambda i, j: (i, 0, 0, 0))),
        out_shape=(jax.ShapeDtypeStruct((b, l, w), BF16),
                   jax.ShapeDtypeStruct((b, REC_HEADS, REC_HEAD_DIM, REC_HEAD_DIM), F32)),
        scratch_shapes=[pltpu.VMEM((REC_HEADS, REC_HEAD_DIM, REC_HEAD_DIM), F32)],
        compiler_params=pltpu.CompilerParams(dimension_semantics=("arbitrary", "arbitrary"),
                                             vmem_limit_bytes=VMEM_LIMIT),
        name="gla_prompt",
    )(rq, zf, rv, rg, lb_logits, nw_t)


def _rec_step_kernel(q_ref, zf_ref, v_ref, g_ref, s0_ref, lbl_ref, nw_ref, o_ref, s_ref):
    lb = _lower_bound(lbl_ref[...])
    zf = zf_ref[...]
    f = lb + (1.0 - lb) * _sigmoid(zf)
    k = (1.0 - lb) * _sigmoid(-zf)
    q = q_ref[...].astype(F32)
    v = v_ref[...].astype(F32)
    g = g_ref[...].astype(F32)
    nw = nw_ref[...]
    d = REC_HEAD_DIM
    ri = lax.broadcasted_iota(jnp.int32, (d, d), 0)
    for h in range(REC_HEADS):
        ln = slice(h * d, (h + 1) * d)
        m = jnp.where(ri == 0, f[:, ln], jnp.where(ri == 1, k[:, ln], jnp.where(ri == 2, q[:, ln], 0.0)))
        mt = m.T
        s_new = mt[:, 0:1] * s0_ref[h] + mt[:, 1:2] * v[:, ln]
        s_ref[h] = s_new
        o = jnp.sum(mt[:, 2:3] * s_new, axis=0, keepdims=True)
        o_ref[:, ln] = _rec_out(o, nw[:, ln], g[:, ln]).astype(o_ref.dtype)


def _rec_step(rq, zf, rv, rg, s0, lb_logits, nw_t):
    b = rq.shape[0]
    w = rq.shape[-1]
    tok = pl.BlockSpec((None, 1, w), lambda i: (i, 0, 0))
    st = pl.BlockSpec((None, REC_HEADS, REC_HEAD_DIM, REC_HEAD_DIM), lambda i: (i, 0, 0, 0))
    full = lambda a: pl.BlockSpec(a.shape, lambda i: (0,) * a.ndim)
    return pl.pallas_call(
        _rec_step_kernel,
        grid=(b,),
        in_specs=[tok, tok, tok, tok, st, full(lb_logits), full(nw_t)],
        out_specs=(tok, st),
        out_shape=(jax.ShapeDtypeStruct((b, 1, w), BF16),
                   jax.ShapeDtypeStruct(s0.shape, F32)),
        compiler_params=pltpu.CompilerParams(dimension_semantics=("arbitrary",),
                                             vmem_limit_bytes=VMEM_LIMIT),
        name="rec_step",
    )(rq, zf, rv, rg, s0, lb_logits, nw_t)


def _subln(o, w, lam_init):
    ms = jnp.mean(o * o, axis=-1, keepdims=True)
    return o * lax.rsqrt(ms + EPS) * w * (1.0 - lam_init)


def _attn_prompt_kernel(lam_init, lp_ref, q_ref, k_ref, v_ref, sw_ref, o_ref,
                        m_ref, l_ref, acc_ref):
    i = pl.program_id(2)
    tq = q_ref.shape[0]
    tk = tq
    lam = _lam_from_params(lp_ref[...], lam_init)
    q = q_ref[...]
    lane = lax.broadcasted_iota(jnp.int32, q.shape, 1)
    first = lane < ATT_HEAD_DIM
    zero = jnp.zeros_like(q)
    qc = (jnp.where(first, q, zero), jnp.where(first, zero, q))
    neg = jnp.finfo(F32).min
    m_ref[...] = jnp.full_like(m_ref, neg)
    l_ref[...] = jnp.zeros_like(l_ref)
    acc_ref[...] = jnp.zeros_like(acc_ref)

    def step(j, masked):
        start = pl.multiple_of(j * tk, tk)
        kj = k_ref[pl.ds(start, tk), :].astype(BF16)
        vj = v_ref[pl.ds(start, tk), :].astype(BF16)
        if masked:
            ri = lax.broadcasted_iota(jnp.int32, (tq, tk), 0)
            ci = lax.broadcasted_iota(jnp.int32, (tq, tk), 1)
            keep = ri >= ci
        for c in range(2):
            s = _dot_nt(qc[c], kj)
            if masked:
                s = jnp.where(keep, s, neg)
            m_old = m_ref[c]
            m_new = jnp.maximum(m_old, jnp.max(s, axis=-1, keepdims=True))
            p = jnp.exp(s - m_new)
            alpha = jnp.exp(m_old - m_new)
            l_ref[c] = alpha * l_ref[c] + jnp.sum(p, axis=-1, keepdims=True)
            acc_ref[c] = alpha * acc_ref[c] + _dot(p.astype(BF16), vj)
            m_ref[c] = m_new

    def body(j, carry):
        step(j, False)
        return carry

    lax.fori_loop(0, i, body, 0)
    step(i, True)
    o = acc_ref[0] / l_ref[0] - lam * (acc_ref[1] / l_ref[1])
    o_ref[...] = _subln(o, sw_ref[...], lam_init).astype(o_ref.dtype)


def _attn_prompt(lam_params, q, k, v, subln_w, lam_init):
    b, l, w = q.shape
    tq = min(ATT_TILE, l)
    assert l % tq == 0
    hd = ATT_V_DIM
    qspec = pl.BlockSpec((None, tq, hd), lambda bi, h, i: (bi, i, h))
    kvspec = pl.BlockSpec((None, l, hd), lambda bi, h, i: (bi, 0, h))
    full = lambda a: pl.BlockSpec(a.shape, lambda bi, h, i: (0,) * a.ndim)
    return pl.pallas_call(
        functools.partial(_attn_prompt_kernel, lam_init),
        grid=(b, ATT_HEADS, l // tq),
        in_specs=[full(lam_params), qspec, kvspec, kvspec, full(subln_w)],
        out_specs=qspec,
        out_shape=jax.ShapeDtypeStruct((b, l, w), BF16),
        scratch_shapes=[pltpu.VMEM((2, tq, 1), F32), pltpu.VMEM((2, tq, 1), F32),
                        pltpu.VMEM((2, tq, hd), F32)],
        compiler_params=pltpu.CompilerParams(
            dimension_semantics=("arbitrary", "arbitrary", "arbitrary"),
            vmem_limit_bytes=VMEM_LIMIT),
        name="attn_prompt",
    )(lam_params, q, k, v, subln_w)


def _head_expand(x):
    rows = []
    for c in range(2):
        parts = [jnp.broadcast_to(x[:, 2 * h + c:2 * h + c + 1], (1, ATT_V_DIM)) for h in range(ATT_HEADS)]
        rows.append(jnp.concatenate(parts, axis=1))
    return rows


def _attn_sample_kernel(lam_init, pt_ref, lp_ref, q_ref, kn_ref, vn_ref, kc_ref, vc_ref,
                        seg_ref, ex_ref, sw_ref, o_ref, m_ref, l_ref, acc_ref):
    del pt_ref
    p_idx = pl.program_id(1)
    q = q_ref[...].astype(F32)
    seg = seg_ref[...]

    @pl.when(p_idx == 0)
    def _():
        kq_new = jnp.broadcast_to(kn_ref[...] * q, (8, ATT_WIDTH))
        m_ref[...] = _split_dot(kq_new, seg, 3)[0:1, :]
        l_ref[...] = jnp.ones_like(l_ref)
        sub = lax.broadcasted_iota(jnp.int32, (8, ATT_WIDTH), 0)
        first = jnp.where(sub == 0, vn_ref[...], 0.0)
        acc_ref[0] = first
        acc_ref[1] = first

    kq = (kc_ref[...] * q).astype(BF16)
    s = _dot(kq, seg)
    m_old = m_ref[...]
    m_new = jnp.maximum(m_old, jnp.max(s, axis=0, keepdims=True))
    p = jnp.exp(s - m_new)
    alpha = jnp.exp(m_old - m_new)
    l_ref[...] = alpha * l_ref[...] + jnp.sum(p, axis=0, keepdims=True)
    m_ref[...] = m_new
    pb = p.astype(BF16)
    alpha_c = _head_expand(alpha)
    v = vc_ref[...]
    n_keys = v.shape[0]
    for c in range(2):
        pe = _dot(pb, ex_ref[c])
        part = jnp.sum((pe * v).reshape(n_keys // 8, 8, ATT_WIDTH), axis=0)
        acc_ref[c] = alpha_c[c] * acc_ref[c] + part

    @pl.when(p_idx == pl.num_programs(1) - 1)
    def _():
        lam = _lam_from_params(lp_ref[...], lam_init)
        l_c = _head_expand(l_ref[...])
        o0 = jnp.sum(acc_ref[0], axis=0, keepdims=True) / l_c[0]
        o1 = jnp.sum(acc_ref[1], axis=0, keepdims=True) / l_c[1]
        o = o0 - lam * o1
        sw = sw_ref[...]
        for h in range(ATT_HEADS):
            ln = slice(h * ATT_V_DIM, (h + 1) * ATT_V_DIM)
            o_ref[:, ln] = _subln(o[:, ln], sw, lam_init).astype(o_ref.dtype)


def _attn_sample(lam_params, q, k_new, v_new, cache_k, cache_v, page_table, seg_cols, expand, subln_w,
                 lam_init):
    b = q.shape[0]
    n_pages = page_table.shape[1]
    page = cache_k.shape[1]
    w = ATT_WIDTH
    tok = pl.BlockSpec((None, 1, w), lambda bi, p, pt: (bi, 0, 0))
    pg = pl.BlockSpec((None, page, w), lambda bi, p, pt: (pt[bi, p], 0, 0))
    full = lambda a: pl.BlockSpec(a.shape, lambda bi, p, pt: (0,) * a.ndim)
    grid_spec = pltpu.PrefetchScalarGridSpec(
        num_scalar_prefetch=1,
        grid=(b, n_pages),
        in_specs=[full(lam_params), tok, tok, tok, pg, pg, full(seg_cols), full(expand), full(subln_w)],
        out_specs=tok,
        scratch_shapes=[pltpu.VMEM((1, LANES), F32), pltpu.VMEM((1, LANES), F32),
                        pltpu.VMEM((2, 8, w), F32)],
    )
    return pl.pallas_call(
        functools.partial(_attn_sample_kernel, lam_init),
        grid_spec=grid_spec,
        out_shape=jax.ShapeDtypeStruct((b, 1, w), BF16),
        compiler_params=pltpu.CompilerParams(dimension_semantics=("arbitrary", "arbitrary"),
                                             vmem_limit_bytes=VMEM_LIMIT),
        name="attn_sample",
    )(page_table, lam_params, q, k_new, v_new, cache_k, cache_v, seg_cols, expand, subln_w)


def _mix_kernel(x_ref, orec_ref, oatt_ref, gate_ref, wur_ref, wua_ref, wo_ref, ln2_ref, wr_ref, br_ref,
                hp_ref, hn_ref, route_ref):
    y_rec = _dot(orec_ref[...], wur_ref[...])
    y_att = _dot(oatt_ref[...], wua_ref[...])
    g_rec = gate_ref[:, :D_MODEL].astype(F32)
    g_att = gate_ref[:, D_MODEL:].astype(F32)
    mix = (g_rec * y_rec + g_att * y_att).astype(BF16)
    hp = x_ref[...] + _dot(mix, wo_ref[...])
    hp_ref[...] = hp
    ms = jnp.mean(hp * hp, axis=-1, keepdims=True)
    hn = (hp * lax.rsqrt(ms + EPS) * ln2_ref[...]).astype(BF16)
    hn_ref[...] = hn
    logits = _dot(hn, wr_ref[...]) + br_ref[...]
    lane = lax.broadcasted_iota(jnp.int32, logits.shape, 1).astype(F32)
    vals, idxs = [], []
    for _ in range(TOP_K):
        mk = jnp.max(logits, axis=-1, keepdims=True)
        ik = jnp.min(jnp.where(logits == mk, lane, float(LANES)), axis=-1, keepdims=True)
        vals.append(mk)
        idxs.append(ik)
        logits = jnp.where(lane == ik, -jnp.inf, logits)
    es = [jnp.exp(v - vals[0]) for v in vals]
    denom = es[0] + es[1] + es[2] + es[3]
    out = jnp.zeros_like(logits)
    for kk in range(TOP_K):
        out = jnp.where(lane == float(kk), idxs[kk], out)
        out = jnp.where(lane == float(TOP_K + kk), es[kk] / denom, out)
    route_ref[...] = out


def _mix(x2d, orec, oatt, gates, wur, wua, wo, ln2, wr, br):
    t = x2d.shape[0]
    tm = min(TOKEN_TILE, t)
    assert t % tm == 0
    row = lambda w: pl.BlockSpec((tm, w), lambda i: (i, 0))
    full = lambda a: pl.BlockSpec(a.shape, lambda i: (0,) * a.ndim)
    return pl.pallas_call(
        _mix_kernel,
        grid=(t // tm,),
        in_specs=[row(D_MODEL), row(REC_WIDTH), row(ATT_WIDTH), row(2 * D_MODEL),
                  full(wur), full(wua), full(wo), full(ln2), full(wr), full(br)],
        out_specs=(row(D_MODEL), row(D_MODEL), row(LANES)),
        out_shape=(jax.ShapeDtypeStruct((t, D_MODEL), F32),
                   jax.ShapeDtypeStruct((t, D_MODEL), BF16),
                   jax.ShapeDtypeStruct((t, LANES), F32)),
        compiler_params=pltpu.CompilerParams(dimension_semantics=("arbitrary",),
                                             vmem_limit_bytes=VMEM_LIMIT),
        name="mix_route",
    )(x2d, orec, oatt, gates, wur, wua, wo, ln2, wr, br)


def _moe_kernel(te_ref, nu_ref, x_ref, wu_ref, bu_ref, wd_ref, bd_ref, rw_ref, y_ref):
    del te_ref
    i = pl.program_id(0)

    @pl.when(i < nu_ref[0])
    def _():
        h = _dot(x_ref[...], wu_ref[...]) + bu_ref[...]
        glu = jnp.minimum(h[:, :D_FF], SWIGLU_LIMIT)
        lin = jnp.clip(h[:, D_FF:], -SWIGLU_LIMIT, SWIGLU_LIMIT)
        a = glu * _sigmoid(SWIGLU_ALPHA * glu) * (lin + 1.0)
        y = _dot(a.astype(BF16), wd_ref[...]) + bd_ref[...]
        y_ref[...] = (rw_ref[...] * y).astype(y_ref.dtype)

    @pl.when(i >= nu_ref[0])
    def _():
        y_ref[...] = jnp.zeros_like(y_ref)


def _moe_ffn(tile_expert, n_used, x_sorted, wu, bu, wd, bd, row_w):
    n_rows = x_sorted.shape[0]
    tm = MOE_TILE
    row = lambda w: pl.BlockSpec((tm, w), lambda i, te, nu: (i, 0))
    per_e = lambda r, c: pl.BlockSpec((None, r, c), lambda i, te, nu: (te[i], 0, 0))
    grid_spec = pltpu.PrefetchScalarGridSpec(
        num_scalar_prefetch=2,
        grid=(n_rows // tm,),
        in_specs=[row(D_MODEL), per_e(D_MODEL, 2 * D_FF), per_e(1, 2 * D_FF),
                  per_e(D_FF, D_MODEL), per_e(1, D_MODEL), row(1)],
        out_specs=row(D_MODEL),
    )
    return pl.pallas_call(
        _moe_kernel,
        grid_spec=grid_spec,
        out_shape=jax.ShapeDtypeStruct((n_rows, D_MODEL), BF16),
        compiler_params=pltpu.CompilerParams(dimension_semantics=("arbitrary",),
                                             vmem_limit_bytes=VMEM_LIMIT),
        name="moe_ffn",
    )(tile_expert, n_used, x_sorted, wu, bu, wd, bd, row_w)


def _combine_kernel(hp_ref, y_ref, o_ref):
    acc = hp_ref[...]
    moe = y_ref[:, 0:D_MODEL].astype(F32)
    for kk in range(1, TOP_K):
        moe = moe + y_ref[:, kk * D_MODEL:(kk + 1) * D_MODEL].astype(F32)
    o_ref[...] = acc + moe


def _combine(hp, y_rows, row_offset):
    t = hp.shape[0]
    tm = min(TOKEN_TILE, t)
    assert t % tm == 0 and row_offset % tm == 0
    off = row_offset // tm
    return pl.pallas_call(
        _combine_kernel,
        grid=(t // tm,),
        in_specs=[pl.BlockSpec((tm, D_MODEL), lambda i: (i, 0)),
                  pl.BlockSpec((tm, TOP_K * D_MODEL), lambda i: (i + off, 0))],
        out_specs=pl.BlockSpec((tm, D_MODEL), lambda i: (i, 0)),
        out_shape=jax.ShapeDtypeStruct((t, D_MODEL), F32),
        compiler_params=pltpu.CompilerParams(dimension_semantics=("arbitrary",),
                                             vmem_limit_bytes=VMEM_LIMIT),
        name="moe_combine",
    )(hp, y_rows)


def _moe(hn_all, route_all, wu, bu, wd, bd):
    t = hn_all.shape[0]
    tm = MOE_TILE
    n = t * TOP_K
    e_flat = route_all[:, :TOP_K].astype(jnp.int32).reshape(n)
    w_flat = route_all[:, TOP_K:2 * TOP_K].reshape(n)
    order = jnp.argsort(e_flat, stable=True).astype(jnp.int32)
    e_sorted = e_flat[order]
    counts = jnp.zeros((N_EXPERTS,), jnp.int32).at[e_flat].add(1)
    padded = ((counts + tm - 1) // tm) * tm
    p_end = jnp.cumsum(padded)
    p_start = p_end - padded
    c_start = jnp.cumsum(counts) - counts
    dest = p_start[e_sorted] + (jnp.arange(n, dtype=jnp.int32) - c_start[e_sorted])
    n_tiles = -(-n // tm) + N_EXPERTS
    n_rows = n_tiles * tm
    src_tok = jnp.zeros((n_rows,), jnp.int32).at[dest].set(order // TOP_K)
    row_w = jnp.zeros((n_rows,), F32).at[dest].set(w_flat[order]).reshape(n_rows, 1)
    pos = jnp.zeros((n,), jnp.int32).at[order].set(dest)
    n_used = (p_end[-1] // tm).astype(jnp.int32).reshape(1)
    tile_start = jnp.arange(n_tiles, dtype=jnp.int32) * tm
    tile_expert = jnp.minimum(jnp.searchsorted(p_end, tile_start, side="right"), N_EXPERTS - 1).astype(jnp.int32)
    last_used = tile_expert[jnp.maximum(n_used[0] - 1, 0)]
    tile_expert = jnp.where(jnp.arange(n_tiles) < n_used[0], tile_expert, last_used)
    x_sorted = jnp.take(hn_all, src_tok, axis=0)
    y_sorted = _moe_ffn(tile_expert, n_used, x_sorted, wu, bu, wd, bd, row_w)
    return jnp.take(y_sorted, pos, axis=0).reshape(t, TOP_K * D_MODEL)


def _seg_mean_matrix():
    g = np.arange(ATT_WIDTH) // ATT_HEAD_DIM
    return jnp.asarray((g[:, None] == g[None, :]).astype(np.float32) / ATT_HEAD_DIM, dtype=BF16)


def _seg_cols_matrix():
    g = np.arange(ATT_WIDTH) // ATT_HEAD_DIM
    return jnp.asarray((g[:, None] == np.arange(LANES)[None, :]).astype(np.float32), dtype=BF16)


def _expand_matrices():
    col_head = np.arange(ATT_WIDTH) // ATT_V_DIM
    m = np.zeros((2, LANES, ATT_WIDTH), np.float32)
    for c in range(2):
        for h in range(ATT_HEADS):
            m[c, 2 * h + c, col_head == h] = 1.0
    return jnp.asarray(m, dtype=BF16)


def kernel(x_prompt, x_sample, cache_k, cache_v, state_rec, page_table, ln1_w, w_in, rec_lb_logits,
           rec_norm_w, w_up_rec, q_norm_w, k_norm_w, lambda_q1, lambda_k1, lambda_q2, lambda_k2,
           att_subln_w, w_up_att, w_out, ln2_w, w_router, b_router, w_exp_up, b_exp_up,
           w_exp_down, b_exp_down):
    depth = ln1_w.shape[0]
    assert depth == 1 and x_sample.shape[1] == 1
    layer = 0
    bp, lp_, d = x_prompt.shape
    bs = x_sample.shape[0]
    tp = bp * lp_
    lam_init = _lambda_init(layer)

    w_in_b = w_in[layer].astype(BF16)
    ln1 = ln1_w[layer].reshape(1, d)
    ln2 = ln2_w[layer].reshape(1, d)
    qn_t = jnp.tile(q_norm_w[layer], ATT_WIDTH // ATT_HEAD_DIM).reshape(1, ATT_WIDTH)
    kn_t = jnp.tile(k_norm_w[layer], ATT_WIDTH // ATT_HEAD_DIM).reshape(1, ATT_WIDTH)
    nw_t = jnp.tile(rec_norm_w[layer], REC_HEADS).reshape(1, REC_WIDTH)
    subln = att_subln_w[layer].reshape(1, ATT_V_DIM)
    lam_params = jnp.stack([lambda_q1[layer], lambda_k1[layer], lambda_q2[layer], lambda_k2[layer]]).astype(F32)
    lb_logits = rec_lb_logits.astype(F32)[layer:layer + 2]
    wur = w_up_rec[layer].astype(BF16)
    wua = w_up_att[layer].astype(BF16)
    wo = w_out[layer].astype(BF16)
    wr = jnp.zeros((d, LANES), BF16).at[:, :N_EXPERTS].set(w_router[layer].astype(BF16))
    br = jnp.full((1, LANES), -1e30, F32).at[0, :N_EXPERTS].set(b_router[layer].astype(F32))
    wu = jnp.concatenate([w_exp_up[layer][:, :, 0::2], w_exp_up[layer][:, :, 1::2]], axis=-1).astype(BF16)
    bu = jnp.concatenate([b_exp_up[layer][:, 0::2], b_exp_up[layer][:, 1::2]], axis=-1).reshape(N_EXPERTS, 1, 2 * D_FF)
    wd = w_exp_down[layer].astype(BF16)
    bd = b_exp_down[layer].reshape(N_EXPERTS, 1, D_MODEL)
    seg_mean = _seg_mean_matrix()
    seg_cols = _seg_cols_matrix()
    expand = _expand_matrices()

    xp2 = x_prompt.reshape(tp, d)
    rq, zf, rv, rg, aq, ak, av, gates = _in_proj(xp2, ln1, w_in_b, seg_mean, qn_t, kn_t)
    seq = lambda a: a.reshape(bp, lp_, a.shape[-1])
    o_rec, s_p = _gla_prompt(seq(rq), seq(zf), seq(rv), seq(rg), lb_logits, nw_t)
    o_att = _attn_prompt(lam_params, seq(aq), seq(ak), seq(av), subln, lam_init)
    hp, hn_p, route_p = _mix(xp2, o_rec.reshape(tp, REC_WIDTH), o_att.reshape(tp, ATT_WIDTH), gates,
                             wur, wua, wo, ln2, wr, br)

    xs2 = x_sample.reshape(bs, d)
    rq_s, zf_s, rv_s, rg_s, aq_s, ak_s, av_s, gates_s = _in_proj(xs2, ln1, w_in_b, seg_mean, qn_t, kn_t)
    tok = lambda a: a.reshape(bs, 1, a.shape[-1])
    o_rec_s, s_s = _rec_step(tok(rq_s), tok(zf_s), tok(rv_s), tok(rg_s), state_rec[layer].astype(F32),
                             lb_logits, nw_t)
    n_pool, page = cache_k.shape[1], cache_k.shape[2]
    ck = cache_k[layer].reshape(n_pool, page, ATT_WIDTH)
    cv = cache_v[layer].reshape(n_pool, page, ATT_WIDTH)
    o_att_s = _attn_sample(lam_params, tok(aq_s), tok(ak_s), tok(av_s), ck, cv, page_table.astype(jnp.int32),
                           seg_cols, expand, subln, lam_init)
    hs, hn_s, route_s = _mix(xs2, o_rec_s.reshape(bs, REC_WIDTH), o_att_s.reshape(bs, ATT_WIDTH), gates_s,
                             wur, wua, wo, ln2, wr, br)

    hn_all = jnp.concatenate([hn_p, hn_s], axis=0)
    route_all = jnp.concatenate([route_p, route_s], axis=0)
    y_rows = _moe(hn_all, route_all, wu, bu, wd, bd)
    y_prompt = _combine(hp, y_rows, 0).reshape(bp, lp_, d)
    y_sample = _combine(hs, y_rows, tp).reshape(bs, 1, d)

    new_k_prompt = ak.reshape(1, bp, lp_, ATT_HEADS, 2, ATT_HEAD_DIM)
    new_v_prompt = av.reshape(1, bp, lp_, ATT_HEADS, ATT_V_DIM)
    new_rec_prompt = s_p.reshape(1, bp, REC_HEADS, REC_HEAD_DIM, REC_HEAD_DIM).astype(state_rec.dtype)
    new_k_sample = ak_s.reshape(1, bs, 1, ATT_HEADS, 2, ATT_HEAD_DIM)
    new_v_sample = av_s.reshape(1, bs, 1, ATT_HEADS, ATT_V_DIM)
    new_rec_sample = s_s.reshape(1, bs, REC_HEADS, REC_HEAD_DIM, REC_HEAD_DIM).astype(state_rec.dtype)
    return (y_prompt, y_sample, new_k_prompt, new_v_prompt, new_rec_prompt,
            new_k_sample, new_v_sample, new_rec_sample)
```

```python
import functools
import math

import numpy as np
import jax
import jax.numpy as jnp
from jax import lax
from jax.experimental import pallas as pl
from jax.experimental.pallas import tpu as pltpu

F32 = jnp.float32
BF16 = jnp.bfloat16

D_MODEL = 1024
REC_WIDTH = 512
REC_HEAD_DIM = 128
REC_HEADS = 4
ATT_WIDTH = 512
ATT_HEAD_DIM = 64
ATT_V_DIM = 128
ATT_HEADS = 4
N_EXPERTS = 32
TOP_K = 4
D_FF = 1024
SWIGLU_LIMIT = 7.0
SWIGLU_ALPHA = 1.702
CHUNK = 64
EPS = 1e-6
N_IN = 4 * REC_WIDTH + 3 * ATT_WIDTH + 2 * D_MODEL
LANES = 128
SUBLANES = 8
MXU_DIM = 256
VMEM_LIMIT = 48 * 1024 * 1024
LOG2E = math.log2(math.e)
Q_SCALE = (ATT_HEAD_DIM ** -0.5) * LOG2E

TOKEN_TILE = 512
GLA_BLOCK = 256
ATT_TILE = 512
MOE_TILE = 512
PAGE_GROUP = 8


def _sigmoid(x):
    return 1.0 / (1.0 + jnp.exp(-x))


def _dot(a, b):
    return jnp.dot(a, b, preferred_element_type=F32)


def _dot_nt(a, b):
    return lax.dot_general(a, b, (((1,), (1,)), ((), ())), preferred_element_type=F32)


def _dot_tn(a, b):
    return lax.dot_general(a, b, (((0,), (0,)), ((), ())), preferred_element_type=F32)


def _split_dot(x, m_bf16, terms):
    acc = None
    r = x
    for t in range(terms):
        p = r.astype(BF16)
        d = _dot(p, m_bf16)
        acc = d if acc is None else acc + d
        if t + 1 < terms:
            r = r - p.astype(F32)
    return acc


def _lambda_init(layer):
    return 0.8 - 0.6 * math.exp(-0.3 * layer)


def _lam_from_params(lp, lam_init):
    a = jnp.sum(lp[0:1, :] * lp[1:2, :], axis=-1, keepdims=True)
    b = jnp.sum(lp[2:3, :] * lp[3:4, :], axis=-1, keepdims=True)
    return jnp.exp(a) - jnp.exp(b) + lam_init


def _lower_bound(lbl):
    a0 = lbl[0:1, :]
    a1 = lbl[1:2, :]
    mx = jnp.maximum(a0, a1)
    e0 = jnp.exp(a0 - mx)
    e1 = jnp.exp(a1 - mx)
    return e0 / (e0 + e1)


def _in_proj_kernel(x_ref, ln_ref, w_ref, seg_ref, qn_ref, kn_ref,
                    rq_ref, zf_ref, rv_ref, rg_ref, aq_ref, ak_ref, av_ref, gate_ref):
    x = x_ref[...]
    ms = jnp.mean(x * x, axis=-1, keepdims=True)
    xn = (x * lax.rsqrt(ms + EPS) * ln_ref[...]).astype(BF16)

    def proj(lo, hi):
        return _dot(xn, w_ref[:, lo:hi])

    def seg_norm(a, w):
        ms_seg = _split_dot(a * a, seg_ref[...], 2)
        return a * lax.rsqrt(ms_seg + EPS) * w

    r = REC_WIDTH
    rq_ref[...] = proj(0, r).astype(BF16)
    zf_ref[...] = proj(r, 2 * r)
    rv_ref[...] = proj(2 * r, 3 * r).astype(BF16)
    rg_ref[...] = proj(3 * r, 4 * r).astype(BF16)
    a0 = 4 * r
    aq = seg_norm(proj(a0, a0 + ATT_WIDTH), qn_ref[...])
    aq_ref[...] = (aq * Q_SCALE).astype(BF16)
    ak_ref[...] = seg_norm(proj(a0 + ATT_WIDTH, a0 + 2 * ATT_WIDTH), kn_ref[...])
    av_ref[...] = proj(a0 + 2 * ATT_WIDTH, a0 + 3 * ATT_WIDTH)
    g0 = a0 + 3 * ATT_WIDTH
    gate_ref[...] = _sigmoid(proj(g0, g0 + 2 * D_MODEL)).astype(BF16)


def _in_proj(x2d, ln_w, w_in_bf16, seg_mat, qn_t, kn_t):
    t = x2d.shape[0]
    tm = min(TOKEN_TILE, t)
    assert t % tm == 0
    row = lambda w: pl.BlockSpec((tm, w), lambda i: (i, 0))
    full = lambda a: pl.BlockSpec(a.shape, lambda i: (0,) * a.ndim)
    out_shapes = (
        jax.ShapeDtypeStruct((t, REC_WIDTH), BF16),
        jax.ShapeDtypeStruct((t, REC_WIDTH), F32),
        jax.ShapeDtypeStruct((t, REC_WIDTH), BF16),
        jax.ShapeDtypeStruct((t, REC_WIDTH), BF16),
        jax.ShapeDtypeStruct((t, ATT_WIDTH), BF16),
        jax.ShapeDtypeStruct((t, ATT_WIDTH), F32),
        jax.ShapeDtypeStruct((t, ATT_WIDTH), F32),
        jax.ShapeDtypeStruct((t, 2 * D_MODEL), BF16),
    )
    return pl.pallas_call(
        _in_proj_kernel,
        grid=(t // tm,),
        in_specs=[row(D_MODEL), full(ln_w), full(w_in_bf16), full(seg_mat), full(qn_t), full(kn_t)],
        out_specs=tuple(row(s.shape[1]) for s in out_shapes),
        out_shape=out_shapes,
        compiler_params=pltpu.CompilerParams(dimension_semantics=("arbitrary",),
                                             vmem_limit_bytes=VMEM_LIMIT),
        name="in_proj",
    )(x2d, ln_w, w_in_bf16, seg_mat, qn_t, kn_t)


def _rec_out(o, nw, g):
    ms = jnp.mean(o * o, axis=-1, keepdims=True)
    return o * lax.rsqrt(ms + EPS) * nw * (g * _sigmoid(g))


def _split_dot_left(m_bf16, x):
    p1 = x.astype(BF16)
    r1 = x - p1.astype(F32)
    p2 = r1.astype(BF16)
    p3 = (r1 - p2.astype(F32)).astype(BF16)
    return _dot(m_bf16, p1) + _dot(m_bf16, p2) + _dot(m_bf16, p3)


def _gla_kernel(q_ref, zf_ref, v_ref, g_ref, lbl_ref, nw_ref, o_ref, sfin_ref, st_ref):
    j = pl.program_id(1)

    @pl.when(j == 0)
    def _():
        st_ref[...] = jnp.zeros_like(st_ref)

    lb = _lower_bound(lbl_ref[...])
    c = CHUNK
    ri = lax.broadcasted_iota(jnp.int32, (c, c), 0)
    ci = lax.broadcasted_iota(jnp.int32, (c, c), 1)
    causal = ri >= ci
    tri = jnp.where(causal, 1.0, 0.0).astype(BF16)
    nw = nw_ref[...]
    n_chunks = q_ref.shape[0] // c
    for n in range(n_chunks):
        rows = slice(n * c, (n + 1) * c)
        zf = zf_ref[rows, :]
        log_f = jnp.log(lb + (1.0 - lb) * _sigmoid(zf))
        k = (1.0 - lb) * _sigmoid(-zf)
        b = _split_dot_left(tri, log_f)
        b_last = b[c - 1:c, :]
        e_pos = jnp.exp(b)
        e_neg = jnp.exp(-b)
        e_tail = jnp.exp(b_last - b)
        decay = jnp.exp(b_last)
        q_e = (q_ref[rows, :].astype(F32) * e_pos).astype(BF16)
        k_e = (k * e_neg).astype(BF16)
        k_t = (k * e_tail).astype(BF16)
        v = v_ref[rows, :]
        g = g_ref[rows, :].astype(F32)
        for h in range(REC_HEADS):
            ln = slice(h * REC_HEAD_DIM, (h + 1) * REC_HEAD_DIM)
            s_t = st_ref[h]
            sc = jnp.where(causal, _dot_nt(q_e[:, ln], k_e[:, ln]), 0.0)
            o = _dot(sc.astype(BF16), v[:, ln]) + _dot_nt(q_e[:, ln], s_t.astype(BF16))
            st_ref[h] = decay[:, ln] * s_t + _dot_tn(v[:, ln], k_t[:, ln])
            o_ref[rows, ln] = _rec_out(o, nw[:, ln], g[:, ln]).astype(o_ref.dtype)

    @pl.when(j == pl.num_programs(1) - 1)
    def _():
        for h in range(REC_HEADS):
            sfin_ref[h] = st_ref[h].T


def _gla_prompt(rq, zf, rv, rg, lb_logits, nw_t):
    b, l, w = rq.shape
    assert l % CHUNK == 0
    lb_rows = min(GLA_BLOCK, l)
    assert l % lb_rows == 0
    seq = pl.BlockSpec((None, lb_rows, w), lambda i, j: (i, j, 0))
    full = lambda a: pl.BlockSpec(a.shape, lambda i, j: (0,) * a.ndim)
    return pl.pallas_call(
        _gla_kernel,
        grid=(b, l // lb_rows),
        in_specs=[seq, seq, seq, seq, full(lb_logits), full(nw_t)],
        out_specs=(seq, pl.BlockSpec((None, REC_HEADS, REC_HEAD_DIM, REC_HEAD_DIM),
                                     lambda i, j: (i, 0, 0, 0))),
        out_shape=(jax.ShapeDtypeStruct((b, l, w), BF16),
                   jax.ShapeDtypeStruct((b, REC_HEADS, REC_HEAD_DIM, REC_HEAD_DIM), F32)),
        scratch_shapes=[pltpu.VMEM((REC_HEADS, REC_HEAD_DIM, REC_HEAD_DIM), F32)],
        compiler_params=pltpu.CompilerParams(dimension_semantics=("arbitrary", "arbitrary"),
                                             vmem_limit_bytes=VMEM_LIMIT),
        name="gla_prompt",
    )(rq, zf, rv, rg, lb_logits, nw_t)


def _rec_step_kernel(q_ref, zf_ref, v_ref, g_ref, s0_ref, lbl_ref, nw_ref, o_ref, s_ref):
    lb = _lower_bound(lbl_ref[...])
    zf = zf_ref[...]
    f = lb + (1.0 - lb) * _sigmoid(zf)
    k = (1.0 - lb) * _sigmoid(-zf)
    q = q_ref[...].astype(F32)
    v = v_ref[...].astype(F32)
    g = g_ref[...].astype(F32)
    nw = nw_ref[...]
    d = REC_HEAD_DIM
    ri = lax.broadcasted_iota(jnp.int32, (d, d), 0)
    for h in range(REC_HEADS):
        ln = slice(h * d, (h + 1) * d)
        m = jnp.where(ri == 0, f[:, ln], jnp.where(ri == 1, k[:, ln], jnp.where(ri == 2, q[:, ln], 0.0)))
        mt = m.T
        s_new = mt[:, 0:1] * s0_ref[h] + mt[:, 1:2] * v[:, ln]
        s_ref[h] = s_new
        o = jnp.sum(mt[:, 2:3] * s_new, axis=0, keepdims=True)
        o_ref[:, ln] = _rec_out(o, nw[:, ln], g[:, ln]).astype(o_ref.dtype)


def _rec_step(rq, zf, rv, rg, s0, lb_logits, nw_t):
    b = rq.shape[0]
    w = rq.shape[-1]
    tok = pl.BlockSpec((None, 1, w), lambda i: (i, 0, 0))
    st = pl.BlockSpec((None, REC_HEADS, REC_HEAD_DIM, REC_HEAD_DIM), lambda i: (i, 0, 0, 0))
    full = lambda a: pl.BlockSpec(a.shape, lambda i: (0,) * a.ndim)
    return pl.pallas_call(
        _rec_step_kernel,
        grid=(b,),
        in_specs=[tok, tok, tok, tok, st, full(lb_logits), full(nw_t)],
        out_specs=(tok, st),
        out_shape=(jax.ShapeDtypeStruct((b, 1, w), BF16),
                   jax.ShapeDtypeStruct(s0.shape, F32)),
        compiler_params=pltpu.CompilerParams(dimension_semantics=("arbitrary",),
                                             vmem_limit_bytes=VMEM_LIMIT),
        name="rec_step",
    )(rq, zf, rv, rg, s0, lb_logits, nw_t)


def _subln(o, w, lam_init):
    ms = jnp.mean(o * o, axis=-1, keepdims=True)
    return o * lax.rsqrt(ms + EPS) * w * (1.0 - lam_init)


def _attn_prompt_kernel(lam_init, lp_ref, q_ref, k_ref, v_ref, sw_ref, o_ref,
                        qs_ref, kb_ref, vb_ref, m_ref, l_ref, acc_ref):
    i = pl.program_id(2)
    tq = q_ref.shape[0]
    tk = tq

    @pl.when(i == 0)
    def _():
        kb_ref[...] = k_ref[...].astype(BF16)
        vb_ref[...] = v_ref[...].astype(BF16)

    q = q_ref[...]
    lane = lax.broadcasted_iota(jnp.int32, q.shape, 1)
    first = lane < ATT_HEAD_DIM
    zero = jnp.zeros_like(q)
    qs_ref[0:tq, :] = jnp.where(first, q, zero)
    qs_ref[tq:2 * tq, :] = jnp.where(first, zero, q)
    neg = jnp.finfo(F32).min
    m_ref[...] = jnp.full_like(m_ref, neg)
    l_ref[...] = jnp.zeros_like(l_ref)
    acc_ref[...] = jnp.zeros_like(acc_ref)

    def step(j, masked):
        start = pl.multiple_of(j * tk, tk)
        kj = kb_ref[pl.ds(start, tk), :]
        vj = vb_ref[pl.ds(start, tk), :]
        s = _dot_nt(qs_ref[...], kj)
        if masked:
            ri = lax.broadcasted_iota(jnp.int32, (2 * tq, tk), 0)
            ci = lax.broadcasted_iota(jnp.int32, (2 * tq, tk), 1)
            ri = jnp.where(ri >= tq, ri - tq, ri)
            s = jnp.where(ri >= ci, s, neg)
        m_old = m_ref[...]
        m_new = jnp.maximum(m_old, jnp.max(s, axis=-1, keepdims=True))
        p = jnp.exp2(s - m_new)
        alpha = jnp.exp2(m_old - m_new)
        l_ref[...] = alpha * l_ref[...] + jnp.sum(p, axis=-1, keepdims=True)
        acc_ref[...] = alpha * acc_ref[...] + _dot(p.astype(BF16), vj)
        m_ref[...] = m_new

    def body(j, carry):
        step(j, False)
        return carry

    lax.fori_loop(0, i, body, 0)
    step(i, True)
    lam = _lam_from_params(lp_ref[...], lam_init)
    o = acc_ref[0:tq, :] / l_ref[0:tq, :] - lam * (acc_ref[tq:2 * tq, :] / l_ref[tq:2 * tq, :])
    o_ref[...] = _subln(o, sw_ref[...], lam_init).astype(o_ref.dtype)


def _attn_prompt(lam_params, q, k, v, subln_w, lam_init):
    b, l, w = q.shape
    tq = min(ATT_TILE, l)
    assert l % tq == 0
    hd = ATT_V_DIM
    qspec = pl.BlockSpec((None, tq, hd), lambda bi, h, i: (bi, i, h))
    kvspec = pl.BlockSpec((None, l, hd), lambda bi, h, i: (bi, 0, h))
    full = lambda a: pl.BlockSpec(a.shape, lambda bi, h, i: (0,) * a.ndim)
    return pl.pallas_call(
        functools.partial(_attn_prompt_kernel, lam_init),
        grid=(b, ATT_HEADS, l // tq),
        in_specs=[full(lam_params), qspec, kvspec, kvspec, full(subln_w)],
        out_specs=qspec,
        out_shape=jax.ShapeDtypeStruct((b, l, w), BF16),
        scratch_shapes=[pltpu.VMEM((2 * tq, hd), BF16), pltpu.VMEM((l, hd), BF16), pltpu.VMEM((l, hd), BF16),
                        pltpu.VMEM((2 * tq, 1), F32), pltpu.VMEM((2 * tq, 1), F32),
                        pltpu.VMEM((2 * tq, hd), F32)],
        compiler_params=pltpu.CompilerParams(
            dimension_semantics=("arbitrary", "arbitrary", "arbitrary"),
            vmem_limit_bytes=VMEM_LIMIT),
        name="attn_prompt",
    )(lam_params, q, k, v, subln_w)


def _seg_scores(feat_by_tok, q_bcast):
    n = feat_by_tok.shape[1]
    prod = feat_by_tok * q_bcast
    return jnp.sum(prod.reshape(2 * ATT_HEADS, ATT_HEAD_DIM, n), axis=1)


def _attn_sample_kernel(lam_init, pt_ref, lp_ref, qb_ref, knb_ref, vn_ref, sw_ref, ck_ref, cv_ref,
                        o_ref, kbuf, vbuf, sem_k, sem_v):
    b = pl.program_id(0)
    nb = pl.num_programs(0)
    n_pages = pt_ref.shape[1]
    grp = kbuf.shape[1]
    n_groups = n_pages // grp
    page = kbuf.shape[3]

    def copies(bi, gi, slot):
        out = []
        for j in range(grp):
            pg = pt_ref[bi, gi * grp + j]
            out.append(pltpu.make_async_copy(ck_ref.at[pg], kbuf.at[slot, j], sem_k.at[slot]))
            out.append(pltpu.make_async_copy(cv_ref.at[pg], vbuf.at[slot, j], sem_v.at[slot]))
        return out

    @pl.when(b == 0)
    def _():
        for cp in copies(0, 0, 0):
            cp.start()

    qb = qb_ref[...]
    m0 = _seg_scores(knb_ref[...], qb)[:, 0:1]
    l0 = jnp.ones_like(m0)
    acc0 = tuple(jnp.broadcast_to(vn_ref[:, h * ATT_V_DIM:(h + 1) * ATT_V_DIM], (2 * ATT_HEADS, ATT_V_DIM))
                 for h in range(ATT_HEADS))

    def body(gi, carry):
        m, l, acc = carry
        slot = lax.rem(gi, 2)
        nxt = gi + 1
        wrap = nxt == n_groups
        nbi = jnp.where(wrap, b + 1, b)
        ngi = jnp.where(wrap, 0, nxt)

        @pl.when(nbi < nb)
        def _():
            for cp in copies(nbi, ngi, 1 - slot):
                cp.start()

        for cp in copies(b, gi, slot):
            cp.wait()
        s = jnp.concatenate([_seg_scores(kbuf[slot, j], qb) for j in range(grp)], axis=1)
        m_new = jnp.maximum(m, jnp.max(s, axis=-1, keepdims=True))
        p = jnp.exp2(s - m_new)
        alpha = jnp.exp2(m - m_new)
        l_new = alpha * l + jnp.sum(p, axis=-1, keepdims=True)
        pb = p.astype(BF16)
        new_acc = []
        for h in range(ATT_HEADS):
            a = alpha * acc[h]
            for j in range(grp):
                vh = vbuf[slot, j, pl.ds(h, page, stride=ATT_HEADS), :].astype(BF16)
                a = a + _dot(pb[:, j * page:(j + 1) * page], vh)
            new_acc.append(a)
        return m_new, l_new, tuple(new_acc)

    m, l, acc = lax.fori_loop(0, n_groups, body, (m0, l0, acc0))
    lam = _lam_from_params(lp_ref[...], lam_init)
    sw = sw_ref[...]
    for h in range(ATT_HEADS):
        r0, r1 = 2 * h, 2 * h + 1
        o = acc[h][r0:r0 + 1, :] / l[r0:r0 + 1, :] - lam * (acc[h][r1:r1 + 1, :] / l[r1:r1 + 1, :])
        o_ref[:, h * ATT_V_DIM:(h + 1) * ATT_V_DIM] = _subln(o, sw, lam_init).astype(o_ref.dtype)


def _attn_sample(lam_params, q_bcast, knew_bcast, v_new, subln_w, cache_kt, cache_vf, page_table, lam_init):
    b = q_bcast.shape[0]
    n_pages = page_table.shape[1]
    page = cache_kt.shape[2]
    grp = min(PAGE_GROUP, n_pages)
    assert n_pages % grp == 0 and (n_pages // grp) % 2 == 0
    w = ATT_WIDTH
    bc = pl.BlockSpec((None, w, LANES), lambda bi, pt: (bi, 0, 0))
    tok = pl.BlockSpec((None, 1, w), lambda bi, pt: (bi, 0, 0))
    full = lambda a: pl.BlockSpec(a.shape, lambda bi, pt: (0,) * a.ndim)
    hbm = pl.BlockSpec(memory_space=pl.ANY)
    grid_spec = pltpu.PrefetchScalarGridSpec(
        num_scalar_prefetch=1,
        grid=(b,),
        in_specs=[full(lam_params), bc, bc, tok, full(subln_w), hbm, hbm],
        out_specs=tok,
        scratch_shapes=[pltpu.VMEM((2, grp, w, page), F32),
                        pltpu.VMEM((2, grp, page * ATT_HEADS, ATT_V_DIM), F32),
                        pltpu.SemaphoreType.DMA((2,)), pltpu.SemaphoreType.DMA((2,))],
    )
    return pl.pallas_call(
        functools.partial(_attn_sample_kernel, lam_init),
        grid_spec=grid_spec,
        out_shape=jax.ShapeDtypeStruct((b, 1, w), BF16),
        compiler_params=pltpu.CompilerParams(dimension_semantics=("arbitrary",),
                                             vmem_limit_bytes=VMEM_LIMIT),
        name="attn_sample",
    )(page_table, lam_params, q_bcast, knew_bcast, v_new, subln_w, cache_kt, cache_vf)


def _mix_kernel(x_ref, orec_ref, oatt_ref, gate_ref, wur_ref, wua_ref, wo_ref, ln2_ref, wr_ref, br_ref,
                hp_ref, hn_ref, route_ref):
    y_rec = _dot(orec_ref[...], wur_ref[...])
    y_att = _dot(oatt_ref[...], wua_ref[...])
    g_rec = gate_ref[:, :D_MODEL].astype(F32)
    g_att = gate_ref[:, D_MODEL:].astype(F32)
    mix = (g_rec * y_rec + g_att * y_att).astype(BF16)
    hp = x_ref[...] + _dot(mix, wo_ref[...])
    hp_ref[...] = hp
    ms = jnp.mean(hp * hp, axis=-1, keepdims=True)
    hn = (hp * lax.rsqrt(ms + EPS) * ln2_ref[...]).astype(BF16)
    hn_ref[...] = hn
    logits = _dot(hn, wr_ref[...]) + br_ref[...]
    lane = lax.broadcasted_iota(jnp.int32, logits.shape, 1).astype(F32)
    vals, idxs = [], []
    for _ in range(TOP_K):
        mk = jnp.max(logits, axis=-1, keepdims=True)
        ik = jnp.min(jnp.where(logits == mk, lane, float(LANES)), axis=-1, keepdims=True)
        vals.append(mk)
        idxs.append(ik)
        logits = jnp.where(lane == ik, -jnp.inf, logits)
    es = [jnp.exp(v - vals[0]) for v in vals]
    denom = es[0] + es[1] + es[2] + es[3]
    out = jnp.zeros_like(logits)
    for kk in range(TOP_K):
        out = jnp.where(lane == float(kk), idxs[kk], out)
        out = jnp.where(lane == float(TOP_K + kk), es[kk] / denom, out)
    route_ref[...] = out


def _mix(x2d, orec, oatt, gates, wur, wua, wo, ln2, wr, br):
    t = x2d.shape[0]
    tm = min(TOKEN_TILE, t)
    assert t % tm == 0
    row = lambda w: pl.BlockSpec((tm, w), lambda i: (i, 0))
    full = lambda a: pl.BlockSpec(a.shape, lambda i: (0,) * a.ndim)
    return pl.pallas_call(
        _mix_kernel,
        grid=(t // tm,),
        in_specs=[row(D_MODEL), row(REC_WIDTH), row(ATT_WIDTH), row(2 * D_MODEL),
                  full(wur), full(wua), full(wo), full(ln2), full(wr), full(br)],
        out_specs=(row(D_MODEL), row(D_MODEL), row(LANES)),
        out_shape=(jax.ShapeDtypeStruct((t, D_MODEL), F32),
                   jax.ShapeDtypeStruct((t, D_MODEL), BF16),
                   jax.ShapeDtypeStruct((t, LANES), F32)),
        compiler_params=pltpu.CompilerParams(dimension_semantics=("arbitrary",),
                                             vmem_limit_bytes=VMEM_LIMIT),
        name="mix_route",
    )(x2d, orec, oatt, gates, wur, wua, wo, ln2, wr, br)


def _pair_split_matrix():
    m = np.zeros((MXU_DIM, MXU_DIM), np.float32)
    half = MXU_DIM // 2
    m[2 * np.arange(half), np.arange(half)] = 1.0
    m[2 * np.arange(half) + 1, half + np.arange(half)] = 1.0
    return m


def _wprep_kernel(w_ref, p_ref, o_ref):
    for g in range(w_ref.shape[1] // MXU_DIM):
        cols = slice(g * MXU_DIM, (g + 1) * MXU_DIM)
        o_ref[:, cols] = _dot(w_ref[:, cols].astype(BF16), p_ref[...]).astype(o_ref.dtype)


def _prep_up_weights(w_up, split_mat):
    e, d, f2 = w_up.shape
    return pl.pallas_call(
        _wprep_kernel,
        grid=(e,),
        in_specs=[pl.BlockSpec((None, d, f2), lambda i: (i, 0, 0)),
                  pl.BlockSpec(split_mat.shape, lambda i: (0, 0))],
        out_specs=pl.BlockSpec((None, d, f2), lambda i: (i, 0, 0)),
        out_shape=jax.ShapeDtypeStruct((e, d, f2), BF16),
        compiler_params=pltpu.CompilerParams(dimension_semantics=("arbitrary",),
                                             vmem_limit_bytes=VMEM_LIMIT),
        name="moe_wprep",
    )(w_up, split_mat)


def _moe_kernel(te_ref, nu_ref, x_ref, wu_ref, bu_ref, wd_ref, bd_ref, rw_ref, y_ref):
    del te_ref
    i = pl.program_id(0)

    @pl.when(i < nu_ref[0])
    def _():
        h = _dot(x_ref[...], wu_ref[...]) + bu_ref[...]
        half = MXU_DIM // 2
        n_grp = h.shape[1] // MXU_DIM
        glu = jnp.concatenate([h[:, g * MXU_DIM:g * MXU_DIM + half] for g in range(n_grp)], axis=1)
        lin = jnp.concatenate([h[:, g * MXU_DIM + half:(g + 1) * MXU_DIM] for g in range(n_grp)], axis=1)
        glu = jnp.minimum(glu, SWIGLU_LIMIT)
        lin = jnp.clip(lin, -SWIGLU_LIMIT, SWIGLU_LIMIT)
        a = glu * _sigmoid(SWIGLU_ALPHA * glu) * (lin + 1.0)
        y = _dot(a.astype(BF16), wd_ref[...]) + bd_ref[...]
        y_ref[...] = (rw_ref[...] * y).astype(y_ref.dtype)

    @pl.when(i >= nu_ref[0])
    def _():
        y_ref[...] = jnp.zeros_like(y_ref)


def _moe_ffn(tile_expert, n_used, x_sorted, wu, bu, wd, bd, row_w):
    n_rows = x_sorted.shape[0]
    tm = MOE_TILE
    row = lambda w: pl.BlockSpec((tm, w), lambda i, te, nu: (i, 0))
    per_e = lambda r, c: pl.BlockSpec((None, r, c), lambda i, te, nu: (te[i], 0, 0))
    grid_spec = pltpu.PrefetchScalarGridSpec(
        num_scalar_prefetch=2,
        grid=(n_rows // tm,),
        in_specs=[row(D_MODEL), per_e(D_MODEL, 2 * D_FF), per_e(1, 2 * D_FF),
                  per_e(D_FF, D_MODEL), per_e(1, D_MODEL), row(1)],
        out_specs=row(D_MODEL),
    )
    return pl.pallas_call(
        _moe_kernel,
        grid_spec=grid_spec,
        out_shape=jax.ShapeDtypeStruct((n_rows, D_MODEL), BF16),
        compiler_params=pltpu.CompilerParams(dimension_semantics=("arbitrary",),
                                             vmem_limit_bytes=VMEM_LIMIT),
        name="moe_ffn",
    )(tile_expert, n_used, x_sorted, wu, bu, wd, bd, row_w)


def _combine_kernel(hp_ref, y_ref, o_ref):
    moe = y_ref[0].astype(F32)
    for kk in range(1, TOP_K):
        moe = moe + y_ref[kk].astype(F32)
    o_ref[...] = hp_ref[...] + moe


def _combine(hp, y_rows, row_offset):
    t = hp.shape[0]
    tm = min(TOKEN_TILE, t)
    assert t % tm == 0 and row_offset % tm == 0
    off = row_offset // tm
    return pl.pallas_call(
        _combine_kernel,
        grid=(t // tm,),
        in_specs=[pl.BlockSpec((tm, D_MODEL), lambda i: (i, 0)),
                  pl.BlockSpec((TOP_K, tm, D_MODEL), lambda i: (0, i + off, 0))],
        out_specs=pl.BlockSpec((tm, D_MODEL), lambda i: (i, 0)),
        out_shape=jax.ShapeDtypeStruct((t, D_MODEL), F32),
        compiler_params=pltpu.CompilerParams(dimension_semantics=("arbitrary",),
                                             vmem_limit_bytes=VMEM_LIMIT),
        name="moe_combine",
    )(hp, y_rows)


def _moe(hn_all, route_all, wu, bu, wd, bd):
    t = hn_all.shape[0]
    tm = MOE_TILE
    n = t * TOP_K
    shift = max(1, (n - 1).bit_length())
    assert N_EXPERTS << shift < 2 ** 31
    e_flat = route_all[:, :TOP_K].astype(jnp.int32).T.reshape(n)
    w_flat = route_all[:, TOP_K:2 * TOP_K].T.reshape(n)
    ar = jnp.arange(n, dtype=jnp.int32)
    keys = jnp.sort((e_flat << shift) | ar)
    order = keys & ((1 << shift) - 1)
    e_sorted = keys >> shift
    c_end = jnp.searchsorted(keys, (jnp.arange(N_EXPERTS, dtype=jnp.int32) + 1) << shift).astype(jnp.int32)
    c_start = jnp.concatenate([jnp.zeros((1,), jnp.int32), c_end[:-1]])
    counts = c_end - c_start
    padded = ((counts + tm - 1) // tm) * tm
    p_end = jnp.cumsum(padded)
    p_start = p_end - padded
    dest_sorted = p_start[e_sorted] + (ar - c_start[e_sorted])
    _, pos = lax.sort((order, dest_sorted), num_keys=1)
    n_tiles = -(-n // tm) + N_EXPERTS
    n_rows = n_tiles * tm
    n_used = (p_end[-1] // tm).astype(jnp.int32).reshape(1)
    tile_start = jnp.arange(n_tiles, dtype=jnp.int32) * tm
    tile_expert = jnp.minimum(jnp.searchsorted(p_end, tile_start, side="right"), N_EXPERTS - 1).astype(jnp.int32)
    last_used = tile_expert[jnp.maximum(n_used[0] - 1, 0)]
    tile_expert = jnp.where(jnp.arange(n_tiles) < n_used[0], tile_expert, last_used)
    e_row = jnp.repeat(tile_expert, tm)
    idx = jnp.arange(n_rows, dtype=jnp.int32) - p_start[e_row]
    valid = idx < counts[e_row]
    asg = order[jnp.clip(c_start[e_row] + idx, 0, n - 1)]
    src_tok = jnp.where(valid, asg % t, 0)
    row_w = jnp.where(valid, w_flat[asg], 0.0).reshape(n_rows, 1)
    x_sorted = hn_all.at[src_tok].get(mode="promise_in_bounds")
    y_sorted = _moe_ffn(tile_expert, n_used, x_sorted, wu, bu, wd, bd, row_w)
    return y_sorted.at[pos].get(mode="promise_in_bounds").reshape(TOP_K, t, D_MODEL)


def _seg_mean_matrix():
    g = np.arange(ATT_WIDTH) // ATT_HEAD_DIM
    return jnp.asarray((g[:, None] == g[None, :]).astype(np.float32) / ATT_HEAD_DIM, dtype=BF16)


def kernel(x_prompt, x_sample, cache_k, cache_v, state_rec, page_table, ln1_w, w_in, rec_lb_logits,
           rec_norm_w, w_up_rec, q_norm_w, k_norm_w, lambda_q1, lambda_k1, lambda_q2, lambda_k2,
           att_subln_w, w_up_att, w_out, ln2_w, w_router, b_router, w_exp_up, b_exp_up,
           w_exp_down, b_exp_down):
    depth = ln1_w.shape[0]
    assert depth == 1 and x_sample.shape[1] == 1
    layer = 0
    bp, lp_, d = x_prompt.shape
    bs = x_sample.shape[0]
    tp = bp * lp_
    lam_init = _lambda_init(layer)

    w_in_b = w_in[layer].astype(BF16)
    ln1 = ln1_w[layer].reshape(1, d)
    ln2 = ln2_w[layer].reshape(1, d)
    qn_t = jnp.tile(q_norm_w[layer], ATT_WIDTH // ATT_HEAD_DIM).reshape(1, ATT_WIDTH)
    kn_t = jnp.tile(k_norm_w[layer], ATT_WIDTH // ATT_HEAD_DIM).reshape(1, ATT_WIDTH)
    nw_t = jnp.tile(rec_norm_w[layer], REC_HEADS).reshape(1, REC_WIDTH)
    subln = att_subln_w[layer].reshape(1, ATT_V_DIM)
    lam_params = jnp.stack([lambda_q1[layer], lambda_k1[layer], lambda_q2[layer], lambda_k2[layer]]).astype(F32)
    lb_logits = rec_lb_logits.astype(F32)[layer:layer + 2]
    wur = w_up_rec[layer].astype(BF16)
    wua = w_up_att[layer].astype(BF16)
    wo = w_out[layer].astype(BF16)
    wr = jnp.zeros((d, LANES), BF16).at[:, :N_EXPERTS].set(w_router[layer].astype(BF16))
    br = jnp.full((1, LANES), -1e30, F32).at[0, :N_EXPERTS].set(b_router[layer].astype(F32))
    wu = _prep_up_weights(w_exp_up[layer], jnp.asarray(_pair_split_matrix(), dtype=BF16))
    half = MXU_DIM // 2
    bu = (b_exp_up[layer].reshape(N_EXPERTS, 2 * D_FF // MXU_DIM, half, 2)
          .transpose(0, 1, 3, 2).reshape(N_EXPERTS, 1, 2 * D_FF))
    wd = w_exp_down[layer].astype(BF16)
    bd = b_exp_down[layer].reshape(N_EXPERTS, 1, D_MODEL)
    seg_mean = _seg_mean_matrix()

    xp2 = x_prompt.reshape(tp, d)
    rq, zf, rv, rg, aq, ak, av, gates = _in_proj(xp2, ln1, w_in_b, seg_mean, qn_t, kn_t)
    seq = lambda a: a.reshape(bp, lp_, a.shape[-1])
    o_rec, s_p = _gla_prompt(seq(rq), seq(zf), seq(rv), seq(rg), lb_logits, nw_t)
    o_att = _attn_prompt(lam_params, seq(aq), seq(ak), seq(av), subln, lam_init)
    hp, hn_p, route_p = _mix(xp2, o_rec.reshape(tp, REC_WIDTH), o_att.reshape(tp, ATT_WIDTH), gates,
                             wur, wua, wo, ln2, wr, br)

    xs2 = x_sample.reshape(bs, d)
    rq_s, zf_s, rv_s, rg_s, aq_s, ak_s, av_s, gates_s = _in_proj(xs2, ln1, w_in_b, seg_mean, qn_t, kn_t)
    tok = lambda a: a.reshape(bs, 1, a.shape[-1])
    o_rec_s, s_s = _rec_step(tok(rq_s), tok(zf_s), tok(rv_s), tok(rg_s), state_rec[layer].astype(F32),
                             lb_logits, nw_t)
    n_pool, page = cache_k.shape[1], cache_k.shape[2]
    ckt = jnp.transpose(cache_k[layer].reshape(n_pool, page, ATT_WIDTH), (0, 2, 1))
    cvf = cache_v[layer].reshape(n_pool, page * ATT_HEADS, ATT_V_DIM)
    q_bcast = jnp.broadcast_to(aq_s.astype(F32)[:, :, None], (bs, ATT_WIDTH, LANES))
    kn_bcast = jnp.broadcast_to(ak_s[:, :, None], (bs, ATT_WIDTH, LANES))
    o_att_s = _attn_sample(lam_params, q_bcast, kn_bcast, tok(av_s), subln, ckt, cvf,
                           page_table.astype(jnp.int32), lam_init)
    hs, hn_s, route_s = _mix(xs2, o_rec_s.reshape(bs, REC_WIDTH), o_att_s.reshape(bs, ATT_WIDTH), gates_s,
                             wur, wua, wo, ln2, wr, br)

    hn_all = jnp.concatenate([hn_p, hn_s], axis=0)
    route_all = jnp.concatenate([route_p, route_s], axis=0)
    y_rows = _moe(hn_all, route_all, wu, bu, wd, bd)
    y_prompt = _combine(hp, y_rows, 0).reshape(bp, lp_, d)
    y_sample = _combine(hs, y_rows, tp).reshape(bs, 1, d)

    new_k_prompt = ak.reshape(1, bp, lp_, ATT_HEADS, 2, ATT_HEAD_DIM)
    new_v_prompt = av.reshape(1, bp, lp_, ATT_HEADS, ATT_V_DIM)
    new_rec_prompt = s_p.reshape(1, bp, REC_HEADS, REC_HEAD_DIM, REC_HEAD_DIM).astype(state_rec.dtype)
    new_k_sample = ak_s.reshape(1, bs, 1, ATT_HEADS, 2, ATT_HEAD_DIM)
    new_v_sample = av_s.reshape(1, bs, 1, ATT_HEADS, ATT_V_DIM)
    new_rec_sample = s_s.reshape(1, bs, REC_HEADS, REC_HEAD_DIM, REC_HEAD_DIM).astype(state_rec.dtype)
    return (y_prompt, y_sample, new_k_prompt, new_v_prompt, new_rec_prompt,
            new_k_sample, new_v_sample, new_rec_sample)
```

```python
import functools
import math

import numpy as np
import jax
import jax.numpy as jnp
from jax import lax
from jax.experimental import pallas as pl
from jax.experimental.pallas import tpu as pltpu

F32 = jnp.float32
BF16 = jnp.bfloat16

D_MODEL = 1024
REC_WIDTH = 512
REC_HEAD_DIM = 128
REC_HEADS = 4
ATT_WIDTH = 512
ATT_HEAD_DIM = 64
ATT_V_DIM = 128
ATT_HEADS = 4
N_EXPERTS = 32
TOP_K = 4
D_FF = 1024
SWIGLU_LIMIT = 7.0
SWIGLU_ALPHA = 1.702
CHUNK = 64
EPS = 1e-6
N_IN = 4 * REC_WIDTH + 3 * ATT_WIDTH + 2 * D_MODEL
LANES = 128
SUBLANES = 8
MXU_DIM = 256
VMEM_LIMIT = 48 * 1024 * 1024
LOG2E = math.log2(math.e)
Q_SCALE = (ATT_HEAD_DIM ** -0.5) * LOG2E

TOKEN_TILE = 512
GLA_BLOCK = 256
ATT_TILE = 512
ATT_ROWS = 256
MOE_TILE = 512
PAGE_GROUP = 8


def _sigmoid(x):
    return 1.0 / (1.0 + jnp.exp(-x))


def _dot(a, b):
    return jnp.dot(a, b, preferred_element_type=F32)


def _dot_nt(a, b):
    return lax.dot_general(a, b, (((1,), (1,)), ((), ())), preferred_element_type=F32)


def _dot_tn(a, b):
    return lax.dot_general(a, b, (((0,), (0,)), ((), ())), preferred_element_type=F32)


def _split_dot(x, m_bf16, terms):
    acc = None
    r = x
    for t in range(terms):
        p = r.astype(BF16)
        d = _dot(p, m_bf16)
        acc = d if acc is None else acc + d
        if t + 1 < terms:
            r = r - p.astype(F32)
    return acc


def _lambda_init(layer):
    return 0.8 - 0.6 * math.exp(-0.3 * layer)


def _lam_from_params(lp, lam_init):
    a = jnp.sum(lp[0:1, :] * lp[1:2, :], axis=-1, keepdims=True)
    b = jnp.sum(lp[2:3, :] * lp[3:4, :], axis=-1, keepdims=True)
    return jnp.exp(a) - jnp.exp(b) + lam_init


def _lower_bound(lbl):
    a0 = lbl[0:1, :]
    a1 = lbl[1:2, :]
    mx = jnp.maximum(a0, a1)
    e0 = jnp.exp(a0 - mx)
    e1 = jnp.exp(a1 - mx)
    return e0 / (e0 + e1)


def _in_proj_kernel(k_feature_major, x_ref, ln_ref, w_ref, seg_ref, qn_ref, kn_ref,
                    rq_ref, zf_ref, rv_ref, rg_ref, aq_ref, ak_ref, av_ref, gate_ref):
    x = x_ref[...]
    ms = jnp.mean(x * x, axis=-1, keepdims=True)
    xn = (x * lax.rsqrt(ms + EPS) * ln_ref[...]).astype(BF16)

    def proj(lo, hi):
        return _dot(xn, w_ref[:, lo:hi])

    def seg_norm(a, w):
        ms_seg = _split_dot(a * a, seg_ref[...], 2)
        return a * lax.rsqrt(ms_seg + EPS) * w

    r = REC_WIDTH
    rq_ref[...] = proj(0, r).astype(BF16)
    zf_ref[...] = proj(r, 2 * r)
    rv_ref[...] = proj(2 * r, 3 * r).astype(BF16)
    rg_ref[...] = proj(3 * r, 4 * r).astype(BF16)
    a0 = 4 * r
    aq = seg_norm(proj(a0, a0 + ATT_WIDTH), qn_ref[...])
    aq_ref[...] = (aq * Q_SCALE).astype(BF16)
    ak = seg_norm(proj(a0 + ATT_WIDTH, a0 + 2 * ATT_WIDTH), kn_ref[...])
    ak_ref[...] = ak.T if k_feature_major else ak
    av_ref[...] = proj(a0 + 2 * ATT_WIDTH, a0 + 3 * ATT_WIDTH)
    g0 = a0 + 3 * ATT_WIDTH
    gate_ref[...] = _sigmoid(proj(g0, g0 + 2 * D_MODEL)).astype(BF16)


def _in_proj(x2d, ln_w, w_in_bf16, seg_mat, qn_t, kn_t, seq_len=None):
    t = x2d.shape[0]
    tm = min(TOKEN_TILE, t)
    assert t % tm == 0
    row = lambda w: pl.BlockSpec((tm, w), lambda i: (i, 0))
    full = lambda a: pl.BlockSpec(a.shape, lambda i: (0,) * a.ndim)
    if seq_len is None:
        k_shape = jax.ShapeDtypeStruct((t, ATT_WIDTH), F32)
        k_spec = row(ATT_WIDTH)
    else:
        assert seq_len % tm == 0 and t % seq_len == 0
        per_seq = seq_len // tm
        k_shape = jax.ShapeDtypeStruct((t // seq_len, ATT_WIDTH, seq_len), F32)
        k_spec = pl.BlockSpec((None, ATT_WIDTH, tm), lambda i: (i // per_seq, 0, i % per_seq))
    out_shapes = (
        jax.ShapeDtypeStruct((t, REC_WIDTH), BF16),
        jax.ShapeDtypeStruct((t, REC_WIDTH), F32),
        jax.ShapeDtypeStruct((t, REC_WIDTH), BF16),
        jax.ShapeDtypeStruct((t, REC_WIDTH), BF16),
        jax.ShapeDtypeStruct((t, ATT_WIDTH), BF16),
        k_shape,
        jax.ShapeDtypeStruct((t, ATT_WIDTH), F32),
        jax.ShapeDtypeStruct((t, 2 * D_MODEL), BF16),
    )
    out_specs = tuple(k_spec if s is k_shape else row(s.shape[1]) for s in out_shapes)
    return pl.pallas_call(
        functools.partial(_in_proj_kernel, seq_len is not None),
        grid=(t // tm,),
        in_specs=[row(D_MODEL), full(ln_w), full(w_in_bf16), full(seg_mat), full(qn_t), full(kn_t)],
        out_specs=out_specs,
        out_shape=out_shapes,
        compiler_params=pltpu.CompilerParams(dimension_semantics=("arbitrary",),
                                             vmem_limit_bytes=VMEM_LIMIT),
        name="in_proj",
    )(x2d, ln_w, w_in_bf16, seg_mat, qn_t, kn_t)


def _rec_out(o, nw, g):
    ms = jnp.mean(o * o, axis=-1, keepdims=True)
    return o * lax.rsqrt(ms + EPS) * nw * (g * _sigmoid(g))


def _split_dot_left(m_bf16, x):
    p1 = x.astype(BF16)
    r1 = x - p1.astype(F32)
    p2 = r1.astype(BF16)
    p3 = (r1 - p2.astype(F32)).astype(BF16)
    return _dot(m_bf16, p1) + _dot(m_bf16, p2) + _dot(m_bf16, p3)


def _gla_kernel(q_ref, zf_ref, v_ref, g_ref, lbl_ref, nw_ref, o_ref, sfin_ref, st_ref):
    j = pl.program_id(1)

    @pl.when(j == 0)
    def _():
        st_ref[...] = jnp.zeros_like(st_ref)

    lb = _lower_bound(lbl_ref[...])
    c = CHUNK
    ri = lax.broadcasted_iota(jnp.int32, (c, c), 0)
    ci = lax.broadcasted_iota(jnp.int32, (c, c), 1)
    causal = ri >= ci
    tri = jnp.where(causal, 1.0, 0.0).astype(BF16)
    nw = nw_ref[...]
    n_chunks = q_ref.shape[0] // c
    for n in range(n_chunks):
        rows = slice(n * c, (n + 1) * c)
        zf = zf_ref[rows, :]
        log_f = jnp.log(lb + (1.0 - lb) * _sigmoid(zf))
        k = (1.0 - lb) * _sigmoid(-zf)
        b = _split_dot_left(tri, log_f)
        b_last = b[c - 1:c, :]
        e_pos = jnp.exp(b)
        e_neg = jnp.exp(-b)
        e_tail = jnp.exp(b_last - b)
        decay = jnp.exp(b_last)
        q_e = (q_ref[rows, :].astype(F32) * e_pos).astype(BF16)
        k_e = (k * e_neg).astype(BF16)
        k_t = (k * e_tail).astype(BF16)
        v = v_ref[rows, :]
        g = g_ref[rows, :].astype(F32)
        for h in range(REC_HEADS):
            ln = slice(h * REC_HEAD_DIM, (h + 1) * REC_HEAD_DIM)
            s_t = st_ref[h]
            sc = jnp.where(causal, _dot_nt(q_e[:, ln], k_e[:, ln]), 0.0)
            o = _dot(sc.astype(BF16), v[:, ln]) + _dot_nt(q_e[:, ln], s_t.astype(BF16))
            st_ref[h] = decay[:, ln] * s_t + _dot_tn(v[:, ln], k_t[:, ln])
            o_ref[rows, ln] = _rec_out(o, nw[:, ln], g[:, ln]).astype(o_ref.dtype)

    @pl.when(j == pl.num_programs(1) - 1)
    def _():
        for h in range(REC_HEADS):
            sfin_ref[h] = st_ref[h].T


def _gla_prompt(rq, zf, rv, rg, lb_logits, nw_t):
    b, l, w = rq.shape
    assert l % CHUNK == 0
    lb_rows = min(GLA_BLOCK, l)
    assert l % lb_rows == 0
    seq = pl.BlockSpec((None, lb_rows, w), lambda i, j: (i, j, 0))
    full = lambda a: pl.BlockSpec(a.shape, lambda i, j: (0,) * a.ndim)
    return pl.pallas_call(
        _gla_kernel,
        grid=(b, l // lb_rows),
        in_specs=[seq, seq, seq, seq, full(lb_logits), full(nw_t)],
        out_specs=(seq, pl.BlockSpec((None, REC_HEADS, REC_HEAD_DIM, REC_HEAD_DIM),
                                     lambda i, j: (i, 0, 0, 0))),
        out_shape=(jax.ShapeDtypeStruct((b, l, w), BF16),
                   jax.ShapeDtypeStruct((b, REC_HEADS, REC_HEAD_DIM, REC_HEAD_DIM), F32)),
        scratch_shapes=[pltpu.VMEM((REC_HEADS, REC_HEAD_DIM, REC_HEAD_DIM), F32)],
        compiler_params=pltpu.CompilerParams(dimension_semantics=("arbitrary", "arbitrary"),
                                             vmem_limit_bytes=VMEM_LIMIT),
        name="gla_prompt",
    )(rq, zf, rv, rg, lb_logits, nw_t)


def _rec_step_kernel(q_ref, zf_ref, v_ref, g_ref, s0_ref, lbl_ref, nw_ref, o_ref, s_ref):
    lb = _lower_bound(lbl_ref[...])
    zf = zf_ref[...]
    f = lb + (1.0 - lb) * _sigmoid(zf)
    k = (1.0 - lb) * _sigmoid(-zf)
    q = q_ref[...].astype(F32)
    v = v_ref[...].astype(F32)
    g = g_ref[...].astype(F32)
    nw = nw_ref[...]
    d = REC_HEAD_DIM
    ri = lax.broadcasted_iota(jnp.int32, (d, d), 0)
    for h in range(REC_HEADS):
        ln = slice(h * d, (h + 1) * d)
        m = jnp.where(ri == 0, f[:, ln], jnp.where(ri == 1, k[:, ln], jnp.where(ri == 2, q[:, ln], 0.0)))
        mt = m.T
        s_new = mt[:, 0:1] * s0_ref[h] + mt[:, 1:2] * v[:, ln]
        s_ref[h] = s_new
        o = jnp.sum(mt[:, 2:3] * s_new, axis=0, keepdims=True)
        o_ref[:, ln] = _rec_out(o, nw[:, ln], g[:, ln]).astype(o_ref.dtype)


def _rec_step(rq, zf, rv, rg, s0, lb_logits, nw_t):
    b = rq.shape[0]
    w = rq.shape[-1]
    tok = pl.BlockSpec((None, 1, w), lambda i: (i, 0, 0))
    st = pl.BlockSpec((None, REC_HEADS, REC_HEAD_DIM, REC_HEAD_DIM), lambda i: (i, 0, 0, 0))
    full = lambda a: pl.BlockSpec(a.shape, lambda i: (0,) * a.ndim)
    return pl.pallas_call(
        _rec_step_kernel,
        grid=(b,),
        in_specs=[tok, tok, tok, tok, st, full(lb_logits), full(nw_t)],
        out_specs=(tok, st),
        out_shape=(jax.ShapeDtypeStruct((b, 1, w), BF16),
                   jax.ShapeDtypeStruct(s0.shape, F32)),
        compiler_params=pltpu.CompilerParams(dimension_semantics=("arbitrary",),
                                             vmem_limit_bytes=VMEM_LIMIT),
        name="rec_step",
    )(rq, zf, rv, rg, s0, lb_logits, nw_t)


def _subln(o, w, lam_init):
    ms = jnp.mean(o * o, axis=-1, keepdims=True)
    return o * lax.rsqrt(ms + EPS) * w * (1.0 - lam_init)


def _lane_block_max(x):
    blocks = [x[:, c * LANES:(c + 1) * LANES] for c in range(x.shape[1] // LANES)]
    return functools.reduce(jnp.maximum, blocks)


def _attn_prompt_kernel(lam_init, lp_ref, q_ref, kt_ref, v_ref, sw_ref, o_ref,
                        qs_ref, kb_ref, vb_ref, acc_ref):
    i = pl.program_id(2)
    tq = q_ref.shape[0]
    seq = kt_ref.shape[1]
    hd = ATT_V_DIM

    @pl.when(i == 0)
    def _():
        kb_ref[...] = kt_ref[...].astype(BF16)
        vb_ref[:, 0:hd] = v_ref[...].astype(BF16)
        vb_ref[:, hd:2 * hd] = jnp.ones((seq, hd), BF16)

    q = q_ref[...]
    lane = lax.broadcasted_iota(jnp.int32, q.shape, 1)
    first = lane < ATT_HEAD_DIM
    zero = jnp.zeros_like(q)
    qs_ref[0:tq, :] = jnp.where(first, q, zero)
    qs_ref[tq:2 * tq, :] = jnp.where(first, zero, q)
    neg = jnp.finfo(F32).min
    rows = min(ATT_ROWS, tq)
    n_chunks = 2 * tq // rows

    def run(n_off):
        def chunk(r, carry):
            row0 = pl.multiple_of(r * rows, rows)
            qc = qs_ref[pl.ds(row0, rows), :]
            base = jnp.where(row0 >= tq, row0 - tq, row0)
            ri = lax.broadcasted_iota(jnp.int32, (rows, tq), 0) + base
            ci = lax.broadcasted_iota(jnp.int32, (rows, tq), 1)
            s_d = jnp.where(ri >= ci, _dot(qc, kb_ref[:, n_off:n_off + tq]), neg)
            mx = _lane_block_max(s_d)
            if n_off:
                s_o = _dot(qc, kb_ref[:, 0:n_off])
                mx = jnp.maximum(mx, _lane_block_max(s_o))
            m = jnp.max(mx, axis=-1, keepdims=True)
            acc = _dot(jnp.exp2(s_d - m).astype(BF16), vb_ref[n_off:n_off + tq, :])
            if n_off:
                acc = acc + _dot(jnp.exp2(s_o - m).astype(BF16), vb_ref[0:n_off, :])
            acc_ref[pl.ds(row0, rows), :] = acc
            return carry

        lax.fori_loop(0, n_chunks, chunk, 0)

    for c in range(seq // tq):
        pl.when(i == c)(functools.partial(run, c * tq))

    lam = _lam_from_params(lp_ref[...], lam_init)
    a0 = acc_ref[0:tq, :]
    a1 = acc_ref[tq:2 * tq, :]
    o = a0[:, 0:hd] / a0[:, hd:2 * hd] - lam * (a1[:, 0:hd] / a1[:, hd:2 * hd])
    o_ref[...] = _subln(o, sw_ref[...], lam_init).astype(o_ref.dtype)


def _attn_prompt(lam_params, q, kt, v, subln_w, lam_init):
    b, l, w = q.shape
    tq = min(ATT_TILE, l)
    assert l % tq == 0
    hd = ATT_V_DIM
    qspec = pl.BlockSpec((None, tq, hd), lambda bi, h, i: (bi, i, h))
    ktspec = pl.BlockSpec((None, hd, l), lambda bi, h, i: (bi, h, 0))
    vspec = pl.BlockSpec((None, l, hd), lambda bi, h, i: (bi, 0, h))
    full = lambda a: pl.BlockSpec(a.shape, lambda bi, h, i: (0,) * a.ndim)
    return pl.pallas_call(
        functools.partial(_attn_prompt_kernel, lam_init),
        grid=(b, ATT_HEADS, l // tq),
        in_specs=[full(lam_params), qspec, ktspec, vspec, full(subln_w)],
        out_specs=qspec,
        out_shape=jax.ShapeDtypeStruct((b, l, w), BF16),
        scratch_shapes=[pltpu.VMEM((2 * tq, hd), BF16), pltpu.VMEM((hd, l), BF16),
                        pltpu.VMEM((l, 2 * hd), BF16), pltpu.VMEM((2 * tq, 2 * hd), F32)],
        compiler_params=pltpu.CompilerParams(
            dimension_semantics=("arbitrary", "arbitrary", "arbitrary"),
            vmem_limit_bytes=VMEM_LIMIT),
        name="attn_prompt",
    )(lam_params, q, kt, v, subln_w)


def _seg_scores(feat_by_tok, q_bcast):
    n = feat_by_tok.shape[1]
    prod = feat_by_tok * q_bcast
    return jnp.sum(prod.reshape(2 * ATT_HEADS, ATT_HEAD_DIM, n), axis=1)


def _attn_sample_kernel(lam_init, pt_ref, lp_ref, qb_ref, knb_ref, vn_ref, sw_ref, ck_ref, cv_ref,
                        o_ref, kbuf, vbuf, sem_k, sem_v):
    b = pl.program_id(0)
    nb = pl.num_programs(0)
    n_pages = pt_ref.shape[1]
    grp = kbuf.shape[1]
    n_groups = n_pages // grp
    page = kbuf.shape[3]

    def copies(bi, gi, slot):
        out = []
        for j in range(grp):
            pg = pt_ref[bi, gi * grp + j]
            out.append(pltpu.make_async_copy(ck_ref.at[pg], kbuf.at[slot, j], sem_k.at[slot]))
            out.append(pltpu.make_async_copy(cv_ref.at[pg], vbuf.at[slot, j], sem_v.at[slot]))
        return out

    @pl.when(b == 0)
    def _():
        for cp in copies(0, 0, 0):
            cp.start()

    qb = qb_ref[...]
    m0 = _seg_scores(knb_ref[...], qb)[:, 0:1]
    l0 = jnp.ones_like(m0)
    acc0 = tuple(jnp.broadcast_to(vn_ref[:, h * ATT_V_DIM:(h + 1) * ATT_V_DIM], (2 * ATT_HEADS, ATT_V_DIM))
                 for h in range(ATT_HEADS))

    def body(gi, carry):
        m, l, acc = carry
        slot = lax.rem(gi, 2)
        nxt = gi + 1
        wrap = nxt == n_groups
        nbi = jnp.where(wrap, b + 1, b)
        ngi = jnp.where(wrap, 0, nxt)

        @pl.when(nbi < nb)
        def _():
            for cp in copies(nbi, ngi, 1 - slot):
                cp.start()

        for cp in copies(b, gi, slot):
            cp.wait()
        s = jnp.concatenate([_seg_scores(kbuf[slot, j], qb) for j in range(grp)], axis=1)
        m_new = jnp.maximum(m, jnp.max(s, axis=-1, keepdims=True))
        p = jnp.exp2(s - m_new)
        alpha = jnp.exp2(m - m_new)
        l_new = alpha * l + jnp.sum(p, axis=-1, keepdims=True)
        pb = p.astype(BF16)
        new_acc = []
        for h in range(ATT_HEADS):
            a = alpha * acc[h]
            for j in range(grp):
                vh = vbuf[slot, j, pl.ds(h, page, stride=ATT_HEADS), :].astype(BF16)
                a = a + _dot(pb[:, j * page:(j + 1) * page], vh)
            new_acc.append(a)
        return m_new, l_new, tuple(new_acc)

    m, l, acc = lax.fori_loop(0, n_groups, body, (m0, l0, acc0))
    lam = _lam_from_params(lp_ref[...], lam_init)
    sw = sw_ref[...]
    for h in range(ATT_HEADS):
        r0, r1 = 2 * h, 2 * h + 1
        o = acc[h][r0:r0 + 1, :] / l[r0:r0 + 1, :] - lam * (acc[h][r1:r1 + 1, :] / l[r1:r1 + 1, :])
        o_ref[:, h * ATT_V_DIM:(h + 1) * ATT_V_DIM] = _subln(o, sw, lam_init).astype(o_ref.dtype)


def _attn_sample(lam_params, q_bcast, knew_bcast, v_new, subln_w, cache_kt, cache_vf, page_table, lam_init):
    b = q_bcast.shape[0]
    n_pages = page_table.shape[1]
    page = cache_kt.shape[2]
    grp = min(PAGE_GROUP, n_pages)
    assert n_pages % grp == 0 and (n_pages // grp) % 2 == 0
    w = ATT_WIDTH
    bc = pl.BlockSpec((None, w, LANES), lambda bi, pt: (bi, 0, 0))
    tok = pl.BlockSpec((None, 1, w), lambda bi, pt: (bi, 0, 0))
    full = lambda a: pl.BlockSpec(a.shape, lambda bi, pt: (0,) * a.ndim)
    hbm = pl.BlockSpec(memory_space=pl.ANY)
    grid_spec = pltpu.PrefetchScalarGridSpec(
        num_scalar_prefetch=1,
        grid=(b,),
        in_specs=[full(lam_params), bc, bc, tok, full(subln_w), hbm, hbm],
        out_specs=tok,
        scratch_shapes=[pltpu.VMEM((2, grp, w, page), F32),
                        pltpu.VMEM((2, grp, page * ATT_HEADS, ATT_V_DIM), F32),
                        pltpu.SemaphoreType.DMA((2,)), pltpu.SemaphoreType.DMA((2,))],
    )
    return pl.pallas_call(
        functools.partial(_attn_sample_kernel, lam_init),
        grid_spec=grid_spec,
        out_shape=jax.ShapeDtypeStruct((b, 1, w), BF16),
        compiler_params=pltpu.CompilerParams(dimension_semantics=("arbitrary",),
                                             vmem_limit_bytes=VMEM_LIMIT),
        name="attn_sample",
    )(page_table, lam_params, q_bcast, knew_bcast, v_new, subln_w, cache_kt, cache_vf)


def _mix_kernel(x_ref, orec_ref, oatt_ref, gate_ref, wur_ref, wua_ref, wo_ref, ln2_ref, wr_ref, br_ref,
                hp_ref, hn_ref, route_ref, hist_ref):
    y_rec = _dot(orec_ref[...], wur_ref[...])
    y_att = _dot(oatt_ref[...], wua_ref[...])
    g_rec = gate_ref[:, :D_MODEL].astype(F32)
    g_att = gate_ref[:, D_MODEL:].astype(F32)
    mix = (g_rec * y_rec + g_att * y_att).astype(BF16)
    hp = x_ref[...] + _dot(mix, wo_ref[...])
    hp_ref[...] = hp
    ms = jnp.mean(hp * hp, axis=-1, keepdims=True)
    hn = (hp * lax.rsqrt(ms + EPS) * ln2_ref[...]).astype(BF16)
    hn_ref[...] = hn
    logits = _dot(hn, wr_ref[...]) + br_ref[...]
    lane = lax.broadcasted_iota(jnp.int32, logits.shape, 1).astype(F32)
    vals, idxs = [], []
    for _ in range(TOP_K):
        mk = jnp.max(logits, axis=-1, keepdims=True)
        ik = jnp.min(jnp.where(logits == mk, lane, float(LANES)), axis=-1, keepdims=True)
        vals.append(mk)
        idxs.append(ik)
        logits = jnp.where(lane == ik, -jnp.inf, logits)
    es = [jnp.exp(v - vals[0]) for v in vals]
    denom = es[0] + es[1] + es[2] + es[3]
    out = jnp.zeros_like(logits)
    hist = jnp.zeros_like(logits)
    for kk in range(TOP_K):
        out = jnp.where(lane == float(kk), idxs[kk], out)
        out = jnp.where(lane == float(TOP_K + kk), es[kk] / denom, out)
        hist = hist + jnp.where(lane == idxs[kk], 1.0, 0.0)
    route_ref[...] = out
    hist_ref[...] = jnp.sum(hist, axis=0, keepdims=True)


def _mix(x2d, orec, oatt, gates, wur, wua, wo, ln2, wr, br):
    t = x2d.shape[0]
    tm = min(TOKEN_TILE, t)
    assert t % tm == 0
    row = lambda w: pl.BlockSpec((tm, w), lambda i: (i, 0))
    full = lambda a: pl.BlockSpec(a.shape, lambda i: (0,) * a.ndim)
    return pl.pallas_call(
        _mix_kernel,
        grid=(t // tm,),
        in_specs=[row(D_MODEL), row(REC_WIDTH), row(ATT_WIDTH), row(2 * D_MODEL),
                  full(wur), full(wua), full(wo), full(ln2), full(wr), full(br)],
        out_specs=(row(D_MODEL), row(D_MODEL), row(LANES),
                   pl.BlockSpec((None, 1, LANES), lambda i: (i, 0, 0))),
        out_shape=(jax.ShapeDtypeStruct((t, D_MODEL), F32),
                   jax.ShapeDtypeStruct((t, D_MODEL), BF16),
                   jax.ShapeDtypeStruct((t, LANES), F32),
                   jax.ShapeDtypeStruct((t // tm, 1, LANES), F32)),
        compiler_params=pltpu.CompilerParams(dimension_semantics=("arbitrary",),
                                             vmem_limit_bytes=VMEM_LIMIT),
        name="mix_route",
    )(x2d, orec, oatt, gates, wur, wua, wo, ln2, wr, br)


def _pair_split_matrix():
    m = np.zeros((MXU_DIM, MXU_DIM), np.float32)
    half = MXU_DIM // 2
    m[2 * np.arange(half), np.arange(half)] = 1.0
    m[2 * np.arange(half) + 1, half + np.arange(half)] = 1.0
    return m


def _wprep_kernel(w_ref, p_ref, o_ref):
    for g in range(w_ref.shape[1] // MXU_DIM):
        cols = slice(g * MXU_DIM, (g + 1) * MXU_DIM)
        o_ref[:, cols] = _dot(w_ref[:, cols].astype(BF16), p_ref[...]).astype(o_ref.dtype)


def _prep_up_weights(w_up, split_mat):
    e, d, f2 = w_up.shape
    return pl.pallas_call(
        _wprep_kernel,
        grid=(e,),
        in_specs=[pl.BlockSpec((None, d, f2), lambda i: (i, 0, 0)),
                  pl.BlockSpec(split_mat.shape, lambda i: (0, 0))],
        out_specs=pl.BlockSpec((None, d, f2), lambda i: (i, 0, 0)),
        out_shape=jax.ShapeDtypeStruct((e, d, f2), BF16),
        compiler_params=pltpu.CompilerParams(dimension_semantics=("arbitrary",),
                                             vmem_limit_bytes=VMEM_LIMIT),
        name="moe_wprep",
    )(w_up, split_mat)


def _moe_kernel(wt_ref, we_ref, lo_ref, hi_ref, first_ref, x_ref, wu_ref, bu_ref, wd_ref, bd_ref, rw_ref,
                y_ref):
    del we_ref
    w = pl.program_id(0)
    lo = lo_ref[w]
    hi = hi_ref[w]

    @pl.when(hi > lo)
    def _():
        h = _dot(x_ref[...], wu_ref[...]) + bu_ref[...]
        half = MXU_DIM // 2
        n_grp = h.shape[1] // MXU_DIM
        glu = jnp.concatenate([h[:, g * MXU_DIM:g * MXU_DIM + half] for g in range(n_grp)], axis=1)
        lin = jnp.concatenate([h[:, g * MXU_DIM + half:(g + 1) * MXU_DIM] for g in range(n_grp)], axis=1)
        glu = jnp.minimum(glu, SWIGLU_LIMIT)
        lin = jnp.clip(lin, -SWIGLU_LIMIT, SWIGLU_LIMIT)
        a = glu * _sigmoid(SWIGLU_ALPHA * glu) * (lin + 1.0)
        y = rw_ref[...] * (_dot(a.astype(BF16), wd_ref[...]) + bd_ref[...])
        tm = y.shape[0]
        row = lax.broadcasted_iota(jnp.int32, (tm, 1), 0) + wt_ref[w] * tm
        y = jnp.where((row >= lo) & (row < hi), y, 0.0).astype(y_ref.dtype)

        @pl.when(first_ref[w] == 1)
        def _():
            y_ref[...] = y

        @pl.when(first_ref[w] == 0)
        def _():
            y_ref[...] = y_ref[...] + y


def _moe_ffn(work, x_sorted, wu, bu, wd, bd, row_w, tm):
    n_rows = x_sorted.shape[0]
    assert n_rows % tm == 0
    row = lambda c: pl.BlockSpec((tm, c), lambda w, wt, we, lo, hi, fi: (wt[w], 0))
    per_e = lambda r, c: pl.BlockSpec((None, r, c), lambda w, wt, we, lo, hi, fi: (we[w], 0, 0))
    grid_spec = pltpu.PrefetchScalarGridSpec(
        num_scalar_prefetch=5,
        grid=(work[0].shape[0],),
        in_specs=[row(D_MODEL), per_e(D_MODEL, 2 * D_FF), per_e(1, 2 * D_FF),
                  per_e(D_FF, D_MODEL), per_e(1, D_MODEL), row(1)],
        out_specs=row(D_MODEL),
    )
    return pl.pallas_call(
        _moe_kernel,
        grid_spec=grid_spec,
        out_shape=jax.ShapeDtypeStruct((n_rows, D_MODEL), BF16),
        compiler_params=pltpu.CompilerParams(dimension_semantics=("arbitrary",),
                                             vmem_limit_bytes=VMEM_LIMIT),
        name="moe_ffn",
    )(*work, x_sorted, wu, bu, wd, bd, row_w)


def _combine_kernel(hp_ref, y_ref, o_ref):
    moe = y_ref[0].astype(F32)
    for kk in range(1, TOP_K):
        moe = moe + y_ref[kk].astype(F32)
    o_ref[...] = hp_ref[...] + moe


def _combine(hp, y_rows):
    t = hp.shape[0]
    tm = min(TOKEN_TILE, t)
    assert t % tm == 0
    return pl.pallas_call(
        _combine_kernel,
        grid=(t // tm,),
        in_specs=[pl.BlockSpec((tm, D_MODEL), lambda i: (i, 0)),
                  pl.BlockSpec((TOP_K, tm, D_MODEL), lambda i: (0, i, 0))],
        out_specs=pl.BlockSpec((tm, D_MODEL), lambda i: (i, 0)),
        out_shape=jax.ShapeDtypeStruct((t, D_MODEL), F32),
        compiler_params=pltpu.CompilerParams(dimension_semantics=("arbitrary",),
                                             vmem_limit_bytes=VMEM_LIMIT),
        name="moe_combine",
    )(hp, y_rows)


def _moe(hn, route, hist, wu, bu, wd, bd):
    t = hn.shape[0]
    n = t * TOP_K
    tm = min(MOE_TILE, t)
    assert n % tm == 0
    n_tiles = n // tm
    shift = max(1, (n - 1).bit_length())
    assert N_EXPERTS << shift < 2 ** 31
    e_flat = route[:, :TOP_K].astype(jnp.int32).reshape(n)
    w_flat = route[:, TOP_K:2 * TOP_K].reshape(n)
    ar = jnp.arange(n, dtype=jnp.int32)
    keys = lax.sort((e_flat << shift) | ar, is_stable=False)
    order = keys & ((1 << shift) - 1)
    _, pos = lax.sort((order, ar), num_keys=1, is_stable=False)
    counts = jnp.sum(hist, axis=(0, 1))[:N_EXPERTS].astype(jnp.int32)
    c_end = jnp.cumsum(counts)
    c_start = c_end - counts
    cuts = lax.sort(jnp.concatenate([jnp.arange(n_tiles, dtype=jnp.int32) * tm, c_start[1:]]), is_stable=False)
    lo = cuts
    hi = jnp.concatenate([cuts[1:], jnp.full((1,), n, jnp.int32)])
    w_tile = jnp.minimum(lo // tm, n_tiles - 1)
    w_expert = jnp.minimum(jnp.sum((c_end[None, :] <= lo[:, None]).astype(jnp.int32), axis=1), N_EXPERTS - 1)
    first = ((lo == w_tile * tm) & (hi > lo)).astype(jnp.int32)
    assert TOP_K == 4
    x_sorted = hn.at[order >> 2].get(mode="promise_in_bounds")
    row_w = w_flat.at[order].get(mode="promise_in_bounds").reshape(n, 1)
    y_sorted = _moe_ffn((w_tile, w_expert, lo, hi, first), x_sorted, wu, bu, wd, bd, row_w, tm)
    pos_k_major = pos.reshape(t, TOP_K).T.reshape(n)
    return y_sorted.at[pos_k_major].get(mode="promise_in_bounds").reshape(TOP_K, t, D_MODEL)


def _seg_mean_matrix():
    g = np.arange(ATT_WIDTH) // ATT_HEAD_DIM
    return jnp.asarray((g[:, None] == g[None, :]).astype(np.float32) / ATT_HEAD_DIM, dtype=BF16)


def kernel(x_prompt, x_sample, cache_k, cache_v, state_rec, page_table, ln1_w, w_in, rec_lb_logits,
           rec_norm_w, w_up_rec, q_norm_w, k_norm_w, lambda_q1, lambda_k1, lambda_q2, lambda_k2,
           att_subln_w, w_up_att, w_out, ln2_w, w_router, b_router, w_exp_up, b_exp_up,
           w_exp_down, b_exp_down):
    depth = ln1_w.shape[0]
    assert depth == 1 and x_sample.shape[1] == 1
    layer = 0
    bp, lp_, d = x_prompt.shape
    bs = x_sample.shape[0]
    tp = bp * lp_
    lam_init = _lambda_init(layer)

    w_in_b = w_in[layer].astype(BF16)
    ln1 = ln1_w[layer].reshape(1, d)
    ln2 = ln2_w[layer].reshape(1, d)
    qn_t = jnp.tile(q_norm_w[layer], ATT_WIDTH // ATT_HEAD_DIM).reshape(1, ATT_WIDTH)
    kn_t = jnp.tile(k_norm_w[layer], ATT_WIDTH // ATT_HEAD_DIM).reshape(1, ATT_WIDTH)
    nw_t = jnp.tile(rec_norm_w[layer], REC_HEADS).reshape(1, REC_WIDTH)
    subln = att_subln_w[layer].reshape(1, ATT_V_DIM)
    lam_params = jnp.stack([lambda_q1[layer], lambda_k1[layer], lambda_q2[layer], lambda_k2[layer]]).astype(F32)
    lb_logits = rec_lb_logits.astype(F32)[layer:layer + 2]
    wur = w_up_rec[layer].astype(BF16)
    wua = w_up_att[layer].astype(BF16)
    wo = w_out[layer].astype(BF16)
    wr = jnp.zeros((d, LANES), BF16).at[:, :N_EXPERTS].set(w_router[layer].astype(BF16))
    br = jnp.full((1, LANES), -1e30, F32).at[0, :N_EXPERTS].set(b_router[layer].astype(F32))
    wu = _prep_up_weights(w_exp_up[layer], jnp.asarray(_pair_split_matrix(), dtype=BF16))
    half = MXU_DIM // 2
    bu = (b_exp_up[layer].reshape(N_EXPERTS, 2 * D_FF // MXU_DIM, half, 2)
          .transpose(0, 1, 3, 2).reshape(N_EXPERTS, 1, 2 * D_FF))
    wd = w_exp_down[layer].astype(BF16)
    bd = b_exp_down[layer].reshape(N_EXPERTS, 1, D_MODEL)
    seg_mean = _seg_mean_matrix()

    xp2 = x_prompt.reshape(tp, d)
    rq, zf, rv, rg, aq, akt, av, gates = _in_proj(xp2, ln1, w_in_b, seg_mean, qn_t, kn_t, seq_len=lp_)
    seq = lambda a: a.reshape(bp, lp_, a.shape[-1])
    o_rec, s_p = _gla_prompt(seq(rq), seq(zf), seq(rv), seq(rg), lb_logits, nw_t)
    o_att = _attn_prompt(lam_params, seq(aq), akt, seq(av), subln, lam_init)
    hp, hn_p, route_p, hist_p = _mix(xp2, o_rec.reshape(tp, REC_WIDTH), o_att.reshape(tp, ATT_WIDTH), gates,
                                     wur, wua, wo, ln2, wr, br)

    xs2 = x_sample.reshape(bs, d)
    rq_s, zf_s, rv_s, rg_s, aq_s, ak_s, av_s, gates_s = _in_proj(xs2, ln1, w_in_b, seg_mean, qn_t, kn_t)
    tok = lambda a: a.reshape(bs, 1, a.shape[-1])
    o_rec_s, s_s = _rec_step(tok(rq_s), tok(zf_s), tok(rv_s), tok(rg_s), state_rec[layer].astype(F32),
                             lb_logits, nw_t)
    n_pool, page = cache_k.shape[1], cache_k.shape[2]
    ckt = jnp.transpose(cache_k[layer].reshape(n_pool, page, ATT_WIDTH), (0, 2, 1))
    cvf = cache_v[layer].reshape(n_pool, page * ATT_HEADS, ATT_V_DIM)
    q_bcast = jnp.broadcast_to(aq_s.astype(F32)[:, :, None], (bs, ATT_WIDTH, LANES))
    kn_bcast = jnp.broadcast_to(ak_s[:, :, None], (bs, ATT_WIDTH, LANES))
    o_att_s = _attn_sample(lam_params, q_bcast, kn_bcast, tok(av_s), subln, ckt, cvf,
                           page_table.astype(jnp.int32), lam_init)
    hs, hn_s, route_s, hist_s = _mix(xs2, o_rec_s.reshape(bs, REC_WIDTH), o_att_s.reshape(bs, ATT_WIDTH),
                                     gates_s, wur, wua, wo, ln2, wr, br)

    y_prompt = _combine(hp, _moe(hn_p, route_p, hist_p, wu, bu, wd, bd)).reshape(bp, lp_, d)
    y_sample = _combine(hs, _moe(hn_s, route_s, hist_s, wu, bu, wd, bd)).reshape(bs, 1, d)

    new_k_prompt = jnp.transpose(akt.reshape(bp, ATT_HEADS, 2, ATT_HEAD_DIM, lp_), (0, 4, 1, 2, 3))[None]
    new_v_prompt = av.reshape(1, bp, lp_, ATT_HEADS, ATT_V_DIM)
    new_rec_prompt = s_p.reshape(1, bp, REC_HEADS, REC_HEAD_DIM, REC_HEAD_DIM).astype(state_rec.dtype)
    new_k_sample = ak_s.reshape(1, bs, 1, ATT_HEADS, 2, ATT_HEAD_DIM)
    new_v_sample = av_s.reshape(1, bs, 1, ATT_HEADS, ATT_V_DIM)
    new_rec_sample = s_s.reshape(1, bs, REC_HEADS, REC_HEAD_DIM, REC_HEAD_DIM).astype(state_rec.dtype)
    return (y_prompt, y_sample, new_k_prompt, new_v_prompt, new_rec_prompt,
            new_k_sample, new_v_sample, new_rec_sample)
```

```python
import functools
import math

import numpy as np
import jax
import jax.numpy as jnp
from jax import lax
from jax.experimental import pallas as pl
from jax.experimental.pallas import tpu as pltpu

F32 = jnp.float32
BF16 = jnp.bfloat16

D_MODEL = 1024
REC_WIDTH = 512
REC_HEAD_DIM = 128
REC_HEADS = 4
ATT_WIDTH = 512
ATT_HEAD_DIM = 64
ATT_V_DIM = 128
ATT_HEADS = 4
N_EXPERTS = 32
TOP_K = 4
D_FF = 1024
SWIGLU_LIMIT = 7.0
SWIGLU_ALPHA = 1.702
CHUNK = 64
EPS = 1e-6
N_IN = 4 * REC_WIDTH + 3 * ATT_WIDTH + 2 * D_MODEL
LANES = 128
SUBLANES = 8
MXU_DIM = 256
VMEM_LIMIT = 48 * 1024 * 1024
LOG2E = math.log2(math.e)
Q_SCALE = (ATT_HEAD_DIM ** -0.5) * LOG2E

TOKEN_TILE = 512
GLA_BLOCK = 512
PAGE_SLOTS = 3
ATT_TILE = 512
ATT_ROWS = 256
MOE_TILE = 512
PAGE_GROUP = 8


def _sigmoid(x):
    return 1.0 / (1.0 + jnp.exp(-x))


def _dot(a, b):
    return jnp.dot(a, b, preferred_element_type=F32)


def _dot_nt(a, b):
    return lax.dot_general(a, b, (((1,), (1,)), ((), ())), preferred_element_type=F32)


def _dot_tn(a, b):
    return lax.dot_general(a, b, (((0,), (0,)), ((), ())), preferred_element_type=F32)


def _split_dot(x, m_bf16, terms):
    acc = None
    r = x
    for t in range(terms):
        p = r.astype(BF16)
        d = _dot(p, m_bf16)
        acc = d if acc is None else acc + d
        if t + 1 < terms:
            r = r - p.astype(F32)
    return acc


def _lambda_init(layer):
    return 0.8 - 0.6 * math.exp(-0.3 * layer)


def _lam_from_params(lp, lam_init):
    a = jnp.sum(lp[0:1, :] * lp[1:2, :], axis=-1, keepdims=True)
    b = jnp.sum(lp[2:3, :] * lp[3:4, :], axis=-1, keepdims=True)
    return jnp.exp(a) - jnp.exp(b) + lam_init


def _lower_bound(lbl):
    a0 = lbl[0:1, :]
    a1 = lbl[1:2, :]
    mx = jnp.maximum(a0, a1)
    e0 = jnp.exp(a0 - mx)
    e1 = jnp.exp(a1 - mx)
    return e0 / (e0 + e1)


def _in_proj_kernel(k_feature_major, x_ref, ln_ref, w_ref, seg_ref, qn_ref, kn_ref,
                    rq_ref, zf_ref, rv_ref, rg_ref, aq_ref, ak_ref, av_ref, gate_ref):
    x = x_ref[...]
    ms = jnp.mean(x * x, axis=-1, keepdims=True)
    xn = (x * lax.rsqrt(ms + EPS) * ln_ref[...]).astype(BF16)

    def proj(lo, hi):
        return _dot(xn, w_ref[:, lo:hi])

    def seg_norm(a, w):
        ms_seg = _split_dot(a * a, seg_ref[...], 2)
        return a * lax.rsqrt(ms_seg + EPS) * w

    r = REC_WIDTH
    rq_ref[...] = proj(0, r).astype(BF16)
    zf_ref[...] = proj(r, 2 * r)
    rv_ref[...] = proj(2 * r, 3 * r).astype(BF16)
    rg_ref[...] = proj(3 * r, 4 * r).astype(BF16)
    a0 = 4 * r
    aq = seg_norm(proj(a0, a0 + ATT_WIDTH), qn_ref[...])
    aq_ref[...] = (aq * Q_SCALE).astype(BF16)
    ak = seg_norm(proj(a0 + ATT_WIDTH, a0 + 2 * ATT_WIDTH), kn_ref[...])
    ak_ref[...] = ak.T if k_feature_major else ak
    av_ref[...] = proj(a0 + 2 * ATT_WIDTH, a0 + 3 * ATT_WIDTH)
    g0 = a0 + 3 * ATT_WIDTH
    gate_ref[...] = _sigmoid(proj(g0, g0 + 2 * D_MODEL)).astype(BF16)


def _in_proj(x2d, ln_w, w_in_bf16, seg_mat, qn_t, kn_t, seq_len=None):
    t = x2d.shape[0]
    tm = min(TOKEN_TILE, t)
    assert t % tm == 0
    row = lambda w: pl.BlockSpec((tm, w), lambda i: (i, 0))
    full = lambda a: pl.BlockSpec(a.shape, lambda i: (0,) * a.ndim)
    if seq_len is None:
        k_shape = jax.ShapeDtypeStruct((t, ATT_WIDTH), F32)
        k_spec = row(ATT_WIDTH)
    else:
        assert seq_len % tm == 0 and t % seq_len == 0
        per_seq = seq_len // tm
        k_shape = jax.ShapeDtypeStruct((t // seq_len, ATT_WIDTH, seq_len), F32)
        k_spec = pl.BlockSpec((None, ATT_WIDTH, tm), lambda i: (i // per_seq, 0, i % per_seq))
    out_shapes = (
        jax.ShapeDtypeStruct((t, REC_WIDTH), BF16),
        jax.ShapeDtypeStruct((t, REC_WIDTH), F32),
        jax.ShapeDtypeStruct((t, REC_WIDTH), BF16),
        jax.ShapeDtypeStruct((t, REC_WIDTH), BF16),
        jax.ShapeDtypeStruct((t, ATT_WIDTH), BF16),
        k_shape,
        jax.ShapeDtypeStruct((t, ATT_WIDTH), F32),
        jax.ShapeDtypeStruct((t, 2 * D_MODEL), BF16),
    )
    out_specs = tuple(k_spec if s is k_shape else row(s.shape[1]) for s in out_shapes)
    return pl.pallas_call(
        functools.partial(_in_proj_kernel, seq_len is not None),
        grid=(t // tm,),
        in_specs=[row(D_MODEL), full(ln_w), full(w_in_bf16), full(seg_mat), full(qn_t), full(kn_t)],
        out_specs=out_specs,
        out_shape=out_shapes,
        compiler_params=pltpu.CompilerParams(dimension_semantics=("arbitrary",),
                                             vmem_limit_bytes=VMEM_LIMIT),
        name="in_proj",
    )(x2d, ln_w, w_in_bf16, seg_mat, qn_t, kn_t)


def _rec_out(o, nw, g):
    ms = jnp.mean(o * o, axis=-1, keepdims=True)
    return o * lax.rsqrt(ms + EPS) * nw * (g * _sigmoid(g))


def _split_dot_left(m_bf16, x):
    p1 = x.astype(BF16)
    r1 = x - p1.astype(F32)
    p2 = r1.astype(BF16)
    p3 = (r1 - p2.astype(F32)).astype(BF16)
    return _dot(m_bf16, p1) + _dot(m_bf16, p2) + _dot(m_bf16, p3)


def _gla_kernel(q_ref, zf_ref, v_ref, g_ref, lbl_ref, nw_ref, o_ref, sfin_ref, st_ref):
    j = pl.program_id(1)

    @pl.when(j == 0)
    def _():
        st_ref[...] = jnp.zeros_like(st_ref)

    lb = _lower_bound(lbl_ref[...])
    c = CHUNK
    ri = lax.broadcasted_iota(jnp.int32, (c, c), 0)
    ci = lax.broadcasted_iota(jnp.int32, (c, c), 1)
    causal = ri >= ci
    tri = jnp.where(causal, 1.0, 0.0).astype(BF16)
    nw = nw_ref[...]
    n_chunks = q_ref.shape[0] // c
    for n in range(n_chunks):
        rows = slice(n * c, (n + 1) * c)
        zf = zf_ref[rows, :]
        log_f = jnp.log(lb + (1.0 - lb) * _sigmoid(zf))
        k = (1.0 - lb) * _sigmoid(-zf)
        b = _split_dot_left(tri, log_f)
        b_last = b[c - 1:c, :]
        e_pos = jnp.exp(b)
        e_neg = jnp.exp(-b)
        e_tail = jnp.exp(b_last - b)
        decay = jnp.exp(b_last)
        q_e = (q_ref[rows, :].astype(F32) * e_pos).astype(BF16)
        k_e = (k * e_neg).astype(BF16)
        k_t = (k * e_tail).astype(BF16)
        v = v_ref[rows, :]
        g = g_ref[rows, :].astype(F32)
        for h in range(REC_HEADS):
            ln = slice(h * REC_HEAD_DIM, (h + 1) * REC_HEAD_DIM)
            s_t = st_ref[h]
            sc = jnp.where(causal, _dot_nt(q_e[:, ln], k_e[:, ln]), 0.0)
            o = _dot(sc.astype(BF16), v[:, ln]) + _dot_nt(q_e[:, ln], s_t.astype(BF16))
            st_ref[h] = decay[:, ln] * s_t + _dot_tn(v[:, ln], k_t[:, ln])
            o_ref[rows, ln] = _rec_out(o, nw[:, ln], g[:, ln]).astype(o_ref.dtype)

    @pl.when(j == pl.num_programs(1) - 1)
    def _():
        for h in range(REC_HEADS):
            sfin_ref[h] = st_ref[h].T


def _gla_prompt(rq, zf, rv, rg, lb_logits, nw_t):
    b, l, w = rq.shape
    assert l % CHUNK == 0
    lb_rows = min(GLA_BLOCK, l)
    assert l % lb_rows == 0
    seq = pl.BlockSpec((None, lb_rows, w), lambda i, j: (i, j, 0))
    full = lambda a: pl.BlockSpec(a.shape, lambda i, j: (0,) * a.ndim)
    return pl.pallas_call(
        _gla_kernel,
        grid=(b, l // lb_rows),
        in_specs=[seq, seq, seq, seq, full(lb_logits), full(nw_t)],
        out_specs=(seq, pl.BlockSpec((None, REC_HEADS, REC_HEAD_DIM, REC_HEAD_DIM),
                                     lambda i, j: (i, 0, 0, 0))),
        out_shape=(jax.ShapeDtypeStruct((b, l, w), BF16),
                   jax.ShapeDtypeStruct((b, REC_HEADS, REC_HEAD_DIM, REC_HEAD_DIM), F32)),
        scratch_shapes=[pltpu.VMEM((REC_HEADS, REC_HEAD_DIM, REC_HEAD_DIM), F32)],
        compiler_params=pltpu.CompilerParams(dimension_semantics=("arbitrary", "arbitrary"),
                                             vmem_limit_bytes=VMEM_LIMIT),
        name="gla_prompt",
    )(rq, zf, rv, rg, lb_logits, nw_t)


def _rec_step_kernel(q_ref, zf_ref, v_ref, g_ref, s0_ref, lbl_ref, nw_ref, o_ref, s_ref):
    lb = _lower_bound(lbl_ref[...])
    zf = zf_ref[...]
    f = lb + (1.0 - lb) * _sigmoid(zf)
    k = (1.0 - lb) * _sigmoid(-zf)
    q = q_ref[...].astype(F32)
    v = v_ref[...].astype(F32)
    g = g_ref[...].astype(F32)
    nw = nw_ref[...]
    d = REC_HEAD_DIM
    ri = lax.broadcasted_iota(jnp.int32, (d, d), 0)
    for h in range(REC_HEADS):
        ln = slice(h * d, (h + 1) * d)
        m = jnp.where(ri == 0, f[:, ln], jnp.where(ri == 1, k[:, ln], jnp.where(ri == 2, q[:, ln], 0.0)))
        mt = m.T
        s_new = mt[:, 0:1] * s0_ref[h] + mt[:, 1:2] * v[:, ln]
        s_ref[h] = s_new
        o = jnp.sum(mt[:, 2:3] * s_new, axis=0, keepdims=True)
        o_ref[:, ln] = _rec_out(o, nw[:, ln], g[:, ln]).astype(o_ref.dtype)


def _rec_step(rq, zf, rv, rg, s0, lb_logits, nw_t):
    b = rq.shape[0]
    w = rq.shape[-1]
    tok = pl.BlockSpec((None, 1, w), lambda i: (i, 0, 0))
    st = pl.BlockSpec((None, REC_HEADS, REC_HEAD_DIM, REC_HEAD_DIM), lambda i: (i, 0, 0, 0))
    full = lambda a: pl.BlockSpec(a.shape, lambda i: (0,) * a.ndim)
    return pl.pallas_call(
        _rec_step_kernel,
        grid=(b,),
        in_specs=[tok, tok, tok, tok, st, full(lb_logits), full(nw_t)],
        out_specs=(tok, st),
        out_shape=(jax.ShapeDtypeStruct((b, 1, w), BF16),
                   jax.ShapeDtypeStruct(s0.shape, F32)),
        compiler_params=pltpu.CompilerParams(dimension_semantics=("arbitrary",),
                                             vmem_limit_bytes=VMEM_LIMIT),
        name="rec_step",
    )(rq, zf, rv, rg, s0, lb_logits, nw_t)


def _subln(o, w, lam_init):
    ms = jnp.mean(o * o, axis=-1, keepdims=True)
    return o * lax.rsqrt(ms + EPS) * w * (1.0 - lam_init)


def _lane_block_max(x):
    blocks = [x[:, c * LANES:(c + 1) * LANES] for c in range(x.shape[1] // LANES)]
    return functools.reduce(jnp.maximum, blocks)


def _attn_prompt_kernel(lam_init, lp_ref, q_ref, kt_ref, v_ref, sw_ref, o_ref,
                        qs_ref, kb_ref, vb_ref, acc_ref):
    i = pl.program_id(2)
    tq = q_ref.shape[0]
    seq = kt_ref.shape[1]
    hd = ATT_V_DIM

    @pl.when(i == 0)
    def _():
        kb_ref[...] = kt_ref[...].astype(BF16)
        vb_ref[:, 0:hd] = v_ref[...].astype(BF16)
        vb_ref[:, hd:2 * hd] = jnp.ones((seq, hd), BF16)

    q = q_ref[...]
    lane = lax.broadcasted_iota(jnp.int32, q.shape, 1)
    first = lane < ATT_HEAD_DIM
    zero = jnp.zeros_like(q)
    qs_ref[0:tq, :] = jnp.where(first, q, zero)
    qs_ref[tq:2 * tq, :] = jnp.where(first, zero, q)
    neg = jnp.finfo(F32).min
    rows = min(ATT_ROWS, tq)
    n_chunks = 2 * tq // rows

    ri = lax.broadcasted_iota(jnp.int32, (rows, rows), 0)
    ci = lax.broadcasted_iota(jnp.int32, (rows, rows), 1)
    causal = ri >= ci

    def run(n_off):
        for r in range(n_chunks):
            row0 = r * rows
            n_full = n_off + row0 % tq
            qc = qs_ref[row0:row0 + rows, :]
            s_d = jnp.where(causal, _dot(qc, kb_ref[:, n_full:n_full + rows]), neg)
            mx = _lane_block_max(s_d)
            if n_full:
                s_o = _dot(qc, kb_ref[:, 0:n_full])
                mx = jnp.maximum(mx, _lane_block_max(s_o))
            m = jnp.max(mx, axis=-1, keepdims=True)
            acc = _dot(jnp.exp2(s_d - m).astype(BF16), vb_ref[n_full:n_full + rows, :])
            if n_full:
                acc = acc + _dot(jnp.exp2(s_o - m).astype(BF16), vb_ref[0:n_full, :])
            acc_ref[row0:row0 + rows, :] = acc

    for c in range(seq // tq):
        pl.when(i == c)(functools.partial(run, c * tq))

    lam = _lam_from_params(lp_ref[...], lam_init)
    a0 = acc_ref[0:tq, :]
    a1 = acc_ref[tq:2 * tq, :]
    o = a0[:, 0:hd] / a0[:, hd:2 * hd] - lam * (a1[:, 0:hd] / a1[:, hd:2 * hd])
    o_ref[...] = _subln(o, sw_ref[...], lam_init).astype(o_ref.dtype)


def _attn_prompt(lam_params, q, kt, v, subln_w, lam_init):
    b, l, w = q.shape
    tq = min(ATT_TILE, l)
    assert l % tq == 0
    hd = ATT_V_DIM
    qspec = pl.BlockSpec((None, tq, hd), lambda bi, h, i: (bi, i, h))
    ktspec = pl.BlockSpec((None, hd, l), lambda bi, h, i: (bi, h, 0))
    vspec = pl.BlockSpec((None, l, hd), lambda bi, h, i: (bi, 0, h))
    full = lambda a: pl.BlockSpec(a.shape, lambda bi, h, i: (0,) * a.ndim)
    return pl.pallas_call(
        functools.partial(_attn_prompt_kernel, lam_init),
        grid=(b, ATT_HEADS, l // tq),
        in_specs=[full(lam_params), qspec, ktspec, vspec, full(subln_w)],
        out_specs=qspec,
        out_shape=jax.ShapeDtypeStruct((b, l, w), BF16),
        scratch_shapes=[pltpu.VMEM((2 * tq, hd), BF16), pltpu.VMEM((hd, l), BF16),
                        pltpu.VMEM((l, 2 * hd), BF16), pltpu.VMEM((2 * tq, 2 * hd), F32)],
        compiler_params=pltpu.CompilerParams(
            dimension_semantics=("arbitrary", "arbitrary", "arbitrary"),
            vmem_limit_bytes=VMEM_LIMIT),
        name="attn_prompt",
    )(lam_params, q, kt, v, subln_w)


def _seg_scores(feat_by_tok, q_bcast):
    n = feat_by_tok.shape[1]
    prod = feat_by_tok * q_bcast
    return jnp.sum(prod.reshape(2 * ATT_HEADS, ATT_HEAD_DIM, n), axis=1)


def _attn_sample_kernel(lam_init, pt_ref, lp_ref, qb_ref, knb_ref, vn_ref, sw_ref, ck_ref, cv_ref,
                        o_ref, kbuf, vbuf, sem_k, sem_v):
    b = pl.program_id(0)
    nb = pl.num_programs(0)
    n_pages = pt_ref.shape[1]
    slots = kbuf.shape[0]
    ahead = slots - 1
    grp = kbuf.shape[1]
    n_groups = n_pages // grp
    page = kbuf.shape[3]
    total = nb * n_groups

    def copies(n):
        bi = n // n_groups
        gi = n % n_groups
        slot = n % slots
        out = []
        for j in range(grp):
            pg = pt_ref[bi, gi * grp + j]
            out.append(pltpu.make_async_copy(ck_ref.at[pg], kbuf.at[slot, j], sem_k.at[slot]))
            out.append(pltpu.make_async_copy(cv_ref.at[pg], vbuf.at[slot, j], sem_v.at[slot]))
        return out

    @pl.when(b == 0)
    def _():
        for n in range(ahead):
            for cp in copies(n):
                cp.start()

    qb = qb_ref[...]
    m0 = _seg_scores(knb_ref[...], qb)[:, 0:1]
    l0 = jnp.ones_like(m0)
    acc0 = tuple(jnp.broadcast_to(vn_ref[:, h * ATT_V_DIM:(h + 1) * ATT_V_DIM], (2 * ATT_HEADS, ATT_V_DIM))
                 for h in range(ATT_HEADS))

    def body(gi, carry):
        m, l, acc = carry
        n = b * n_groups + gi
        slot = n % slots

        @pl.when(n + ahead < total)
        def _():
            for cp in copies(n + ahead):
                cp.start()

        for cp in copies(n):
            cp.wait()
        s = jnp.concatenate([_seg_scores(kbuf[slot, j], qb) for j in range(grp)], axis=1)
        m_new = jnp.maximum(m, jnp.max(s, axis=-1, keepdims=True))
        p = jnp.exp2(s - m_new)
        alpha = jnp.exp2(m - m_new)
        l_new = alpha * l + jnp.sum(p, axis=-1, keepdims=True)
        pb = p.astype(BF16)
        new_acc = []
        for h in range(ATT_HEADS):
            a = alpha * acc[h]
            for j in range(grp):
                vh = vbuf[slot, j, pl.ds(h, page, stride=ATT_HEADS), :].astype(BF16)
                a = a + _dot(pb[:, j * page:(j + 1) * page], vh)
            new_acc.append(a)
        return m_new, l_new, tuple(new_acc)

    m, l, acc = lax.fori_loop(0, n_groups, body, (m0, l0, acc0))
    lam = _lam_from_params(lp_ref[...], lam_init)
    sw = sw_ref[...]
    for h in range(ATT_HEADS):
        r0, r1 = 2 * h, 2 * h + 1
        o = acc[h][r0:r0 + 1, :] / l[r0:r0 + 1, :] - lam * (acc[h][r1:r1 + 1, :] / l[r1:r1 + 1, :])
        o_ref[:, h * ATT_V_DIM:(h + 1) * ATT_V_DIM] = _subln(o, sw, lam_init).astype(o_ref.dtype)


def _attn_sample(lam_params, q_bcast, knew_bcast, v_new, subln_w, cache_kt, cache_vf, page_table, lam_init):
    b = q_bcast.shape[0]
    n_pages = page_table.shape[1]
    page = cache_kt.shape[2]
    grp = min(PAGE_GROUP, n_pages)
    slots = PAGE_SLOTS
    assert n_pages % grp == 0 and b * (n_pages // grp) >= slots - 1
    w = ATT_WIDTH
    bc = pl.BlockSpec((None, w, LANES), lambda bi, pt: (bi, 0, 0))
    tok = pl.BlockSpec((None, 1, w), lambda bi, pt: (bi, 0, 0))
    full = lambda a: pl.BlockSpec(a.shape, lambda bi, pt: (0,) * a.ndim)
    hbm = pl.BlockSpec(memory_space=pl.ANY)
    grid_spec = pltpu.PrefetchScalarGridSpec(
        num_scalar_prefetch=1,
        grid=(b,),
        in_specs=[full(lam_params), bc, bc, tok, full(subln_w), hbm, hbm],
        out_specs=tok,
        scratch_shapes=[pltpu.VMEM((slots, grp, w, page), F32),
                        pltpu.VMEM((slots, grp, page * ATT_HEADS, ATT_V_DIM), F32),
                        pltpu.SemaphoreType.DMA((slots,)), pltpu.SemaphoreType.DMA((slots,))],
    )
    return pl.pallas_call(
        functools.partial(_attn_sample_kernel, lam_init),
        grid_spec=grid_spec,
        out_shape=jax.ShapeDtypeStruct((b, 1, w), BF16),
        compiler_params=pltpu.CompilerParams(dimension_semantics=("arbitrary",),
                                             vmem_limit_bytes=VMEM_LIMIT),
        name="attn_sample",
    )(page_table, lam_params, q_bcast, knew_bcast, v_new, subln_w, cache_kt, cache_vf)


def _mix_kernel(x_ref, orec_ref, oatt_ref, gate_ref, wur_ref, wua_ref, wo_ref, ln2_ref, wr_ref, br_ref,
                hp_ref, hn_ref, route_ref, hist_ref):
    y_rec = _dot(orec_ref[...], wur_ref[...])
    y_att = _dot(oatt_ref[...], wua_ref[...])
    g_rec = gate_ref[:, :D_MODEL].astype(F32)
    g_att = gate_ref[:, D_MODEL:].astype(F32)
    mix = (g_rec * y_rec + g_att * y_att).astype(BF16)
    hp = x_ref[...] + _dot(mix, wo_ref[...])
    hp_ref[...] = hp
    ms = jnp.mean(hp * hp, axis=-1, keepdims=True)
    hn = (hp * lax.rsqrt(ms + EPS) * ln2_ref[...]).astype(BF16)
    hn_ref[...] = hn
    logits = _dot(hn, wr_ref[...]) + br_ref[...]
    lt = logits.T[0:N_EXPERTS, :]
    row = lax.broadcasted_iota(jnp.int32, lt.shape, 0).astype(F32)
    vals, idxs = [], []
    for _ in range(TOP_K):
        mk = jnp.max(lt, axis=0, keepdims=True)
        ik = jnp.min(jnp.where(lt == mk, row, float(N_EXPERTS)), axis=0, keepdims=True)
        vals.append(mk)
        idxs.append(ik)
        lt = jnp.where(row == ik, -jnp.inf, lt)
    es = [jnp.exp(v - vals[0]) for v in vals]
    denom = es[0] + es[1] + es[2] + es[3]
    route_ref[...] = jnp.concatenate(idxs + [e / denom for e in es], axis=0)
    hits = jnp.where(row == idxs[0], 1.0, 0.0)
    for kk in range(1, TOP_K):
        hits = hits + jnp.where(row == idxs[kk], 1.0, 0.0)
    hist_ref[...] = jnp.sum(hits, axis=1, keepdims=True)


def _mix(x2d, orec, oatt, gates, wur, wua, wo, ln2, wr, br):
    t = x2d.shape[0]
    tm = min(TOKEN_TILE, t)
    assert t % tm == 0
    row = lambda w: pl.BlockSpec((tm, w), lambda i: (i, 0))
    full = lambda a: pl.BlockSpec(a.shape, lambda i: (0,) * a.ndim)
    return pl.pallas_call(
        _mix_kernel,
        grid=(t // tm,),
        in_specs=[row(D_MODEL), row(REC_WIDTH), row(ATT_WIDTH), row(2 * D_MODEL),
                  full(wur), full(wua), full(wo), full(ln2), full(wr), full(br)],
        out_specs=(row(D_MODEL), row(D_MODEL),
                   pl.BlockSpec((2 * TOP_K, tm), lambda i: (0, i)),
                   pl.BlockSpec((None, N_EXPERTS, 1), lambda i: (i, 0, 0))),
        out_shape=(jax.ShapeDtypeStruct((t, D_MODEL), F32),
                   jax.ShapeDtypeStruct((t, D_MODEL), BF16),
                   jax.ShapeDtypeStruct((2 * TOP_K, t), F32),
                   jax.ShapeDtypeStruct((t // tm, N_EXPERTS, 1), F32)),
        compiler_params=pltpu.CompilerParams(dimension_semantics=("arbitrary",),
                                             vmem_limit_bytes=VMEM_LIMIT),
        name="mix_route",
    )(x2d, orec, oatt, gates, wur, wua, wo, ln2, wr, br)


def _pair_split_matrix():
    m = np.zeros((MXU_DIM, MXU_DIM), np.float32)
    half = MXU_DIM // 2
    m[2 * np.arange(half), np.arange(half)] = 1.0
    m[2 * np.arange(half) + 1, half + np.arange(half)] = 1.0
    return m


def _wprep_kernel(w_ref, p_ref, o_ref):
    for g in range(w_ref.shape[1] // MXU_DIM):
        cols = slice(g * MXU_DIM, (g + 1) * MXU_DIM)
        o_ref[:, cols] = _dot(w_ref[:, cols].astype(BF16), p_ref[...]).astype(o_ref.dtype)


def _prep_up_weights(w_up, split_mat):
    e, d, f2 = w_up.shape
    return pl.pallas_call(
        _wprep_kernel,
        grid=(e,),
        in_specs=[pl.BlockSpec((None, d, f2), lambda i: (i, 0, 0)),
                  pl.BlockSpec(split_mat.shape, lambda i: (0, 0))],
        out_specs=pl.BlockSpec((None, d, f2), lambda i: (i, 0, 0)),
        out_shape=jax.ShapeDtypeStruct((e, d, f2), BF16),
        compiler_params=pltpu.CompilerParams(dimension_semantics=("arbitrary",),
                                             vmem_limit_bytes=VMEM_LIMIT),
        name="moe_wprep",
    )(w_up, split_mat)


def _moe_kernel(wt_ref, we_ref, lo_ref, hi_ref, first_ref, x_ref, wu_ref, bu_ref, wd_ref, bd_ref, rw_ref,
                y_ref):
    del we_ref
    w = pl.program_id(0)
    lo = lo_ref[w]
    hi = hi_ref[w]

    @pl.when(hi > lo)
    def _():
        h = _dot(x_ref[...], wu_ref[...]) + bu_ref[...]
        half = MXU_DIM // 2
        n_grp = h.shape[1] // MXU_DIM
        glu = jnp.concatenate([h[:, g * MXU_DIM:g * MXU_DIM + half] for g in range(n_grp)], axis=1)
        lin = jnp.concatenate([h[:, g * MXU_DIM + half:(g + 1) * MXU_DIM] for g in range(n_grp)], axis=1)
        glu = jnp.minimum(glu, SWIGLU_LIMIT)
        lin = jnp.clip(lin, -SWIGLU_LIMIT, SWIGLU_LIMIT)
        a = glu * _sigmoid(SWIGLU_ALPHA * glu) * (lin + 1.0)
        y = rw_ref[...] * (_dot(a.astype(BF16), wd_ref[...]) + bd_ref[...])
        tm = y.shape[0]
        row = lax.broadcasted_iota(jnp.int32, (tm, 1), 0) + wt_ref[w] * tm
        y = jnp.where((row >= lo) & (row < hi), y, 0.0).astype(y_ref.dtype)

        @pl.when(first_ref[w] == 1)
        def _():
            y_ref[...] = y

        @pl.when(first_ref[w] == 0)
        def _():
            y_ref[...] = y_ref[...] + y


def _moe_ffn(work, x_sorted, wu, bu, wd, bd, row_w, tm):
    n_rows = x_sorted.shape[0]
    assert n_rows % tm == 0
    row = lambda c: pl.BlockSpec((tm, c), lambda w, wt, we, lo, hi, fi: (wt[w], 0))
    per_e = lambda r, c: pl.BlockSpec((None, r, c), lambda w, wt, we, lo, hi, fi: (we[w], 0, 0))
    grid_spec = pltpu.PrefetchScalarGridSpec(
        num_scalar_prefetch=5,
        grid=(work[0].shape[0],),
        in_specs=[row(D_MODEL), per_e(D_MODEL, 2 * D_FF), per_e(1, 2 * D_FF),
                  per_e(D_FF, D_MODEL), per_e(1, D_MODEL), row(1)],
        out_specs=row(D_MODEL),
    )
    return pl.pallas_call(
        _moe_kernel,
        grid_spec=grid_spec,
        out_shape=jax.ShapeDtypeStruct((n_rows, D_MODEL), BF16),
        compiler_params=pltpu.CompilerParams(dimension_semantics=("arbitrary",),
                                             vmem_limit_bytes=VMEM_LIMIT),
        name="moe_ffn",
    )(*work, x_sorted, wu, bu, wd, bd, row_w)


def _combine_kernel(hp_ref, y_ref, o_ref):
    moe = y_ref[0].astype(F32)
    for kk in range(1, TOP_K):
        moe = moe + y_ref[kk].astype(F32)
    o_ref[...] = hp_ref[...] + moe


def _combine(hp, y_rows):
    t = hp.shape[0]
    tm = min(TOKEN_TILE, t)
    assert t % tm == 0
    return pl.pallas_call(
        _combine_kernel,
        grid=(t // tm,),
        in_specs=[pl.BlockSpec((tm, D_MODEL), lambda i: (i, 0)),
                  pl.BlockSpec((TOP_K, tm, D_MODEL), lambda i: (0, i, 0))],
        out_specs=pl.BlockSpec((tm, D_MODEL), lambda i: (i, 0)),
        out_shape=jax.ShapeDtypeStruct((t, D_MODEL), F32),
        compiler_params=pltpu.CompilerParams(dimension_semantics=("arbitrary",),
                                             vmem_limit_bytes=VMEM_LIMIT),
        name="moe_combine",
    )(hp, y_rows)


def _moe(hn, route, hist, wu, bu, wd, bd):
    t = hn.shape[0]
    n = t * TOP_K
    tm = min(MOE_TILE, t)
    assert n % tm == 0
    n_tiles = n // tm
    shift = max(1, (n - 1).bit_length())
    assert N_EXPERTS << shift < 2 ** 31
    e_flat = route[:TOP_K].astype(jnp.int32).reshape(n)
    w_flat = route[TOP_K:2 * TOP_K].reshape(n)
    ar = jnp.arange(n, dtype=jnp.int32)
    keys = lax.sort((e_flat << shift) | ar, is_stable=False)
    order = keys & ((1 << shift) - 1)
    _, pos = lax.sort((order, ar), num_keys=1, is_stable=False)
    counts = jnp.sum(hist, axis=(0, 2)).astype(jnp.int32)
    c_end = jnp.cumsum(counts)
    c_start = c_end - counts
    cuts = lax.sort(jnp.concatenate([jnp.arange(n_tiles, dtype=jnp.int32) * tm, c_start[1:]]), is_stable=False)
    lo = cuts
    hi = jnp.concatenate([cuts[1:], jnp.full((1,), n, jnp.int32)])
    w_tile = jnp.minimum(lo // tm, n_tiles - 1)
    w_expert = jnp.minimum(jnp.sum((c_end[None, :] <= lo[:, None]).astype(jnp.int32), axis=1), N_EXPERTS - 1)
    first = ((lo == w_tile * tm) & (hi > lo)).astype(jnp.int32)
    token = (order & (t - 1)) if t & (t - 1) == 0 else order % t
    x_sorted = hn.at[token].get(mode="promise_in_bounds")
    row_w = w_flat.at[order].get(mode="promise_in_bounds").reshape(n, 1)
    y_sorted = _moe_ffn((w_tile, w_expert, lo, hi, first), x_sorted, wu, bu, wd, bd, row_w, tm)
    return y_sorted.at[pos].get(mode="promise_in_bounds").reshape(TOP_K, t, D_MODEL)


def _seg_mean_matrix():
    g = np.arange(ATT_WIDTH) // ATT_HEAD_DIM
    return jnp.asarray((g[:, None] == g[None, :]).astype(np.float32) / ATT_HEAD_DIM, dtype=BF16)


def kernel(x_prompt, x_sample, cache_k, cache_v, state_rec, page_table, ln1_w, w_in, rec_lb_logits,
           rec_norm_w, w_up_rec, q_norm_w, k_norm_w, lambda_q1, lambda_k1, lambda_q2, lambda_k2,
           att_subln_w, w_up_att, w_out, ln2_w, w_router, b_router, w_exp_up, b_exp_up,
           w_exp_down, b_exp_down):
    depth = ln1_w.shape[0]
    assert depth == 1 and x_sample.shape[1] == 1
    layer = 0
    bp, lp_, d = x_prompt.shape
    bs = x_sample.shape[0]
    tp = bp * lp_
    lam_init = _lambda_init(layer)

    w_in_b = w_in[layer].astype(BF16)
    ln1 = ln1_w[layer].reshape(1, d)
    ln2 = ln2_w[layer].reshape(1, d)
    qn_t = jnp.tile(q_norm_w[layer], ATT_WIDTH // ATT_HEAD_DIM).reshape(1, ATT_WIDTH)
    kn_t = jnp.tile(k_norm_w[layer], ATT_WIDTH // ATT_HEAD_DIM).reshape(1, ATT_WIDTH)
    nw_t = jnp.tile(rec_norm_w[layer], REC_HEADS).reshape(1, REC_WIDTH)
    subln = att_subln_w[layer].reshape(1, ATT_V_DIM)
    lam_params = jnp.stack([lambda_q1[layer], lambda_k1[layer], lambda_q2[layer], lambda_k2[layer]]).astype(F32)
    lb_logits = rec_lb_logits.astype(F32)[layer:layer + 2]
    wur = w_up_rec[layer].astype(BF16)
    wua = w_up_att[layer].astype(BF16)
    wo = w_out[layer].astype(BF16)
    wr = jnp.zeros((d, LANES), BF16).at[:, :N_EXPERTS].set(w_router[layer].astype(BF16))
    br = jnp.full((1, LANES), -1e30, F32).at[0, :N_EXPERTS].set(b_router[layer].astype(F32))
    wu = _prep_up_weights(w_exp_up[layer], jnp.asarray(_pair_split_matrix(), dtype=BF16))
    half = MXU_DIM // 2
    bu = (b_exp_up[layer].reshape(N_EXPERTS, 2 * D_FF // MXU_DIM, half, 2)
          .transpose(0, 1, 3, 2).reshape(N_EXPERTS, 1, 2 * D_FF))
    wd = w_exp_down[layer].astype(BF16)
    bd = b_exp_down[layer].reshape(N_EXPERTS, 1, D_MODEL)
    seg_mean = _seg_mean_matrix()

    xp2 = x_prompt.reshape(tp, d)
    rq, zf, rv, rg, aq, akt, av, gates = _in_proj(xp2, ln1, w_in_b, seg_mean, qn_t, kn_t, seq_len=lp_)
    seq = lambda a: a.reshape(bp, lp_, a.shape[-1])
    o_rec, s_p = _gla_prompt(seq(rq), seq(zf), seq(rv), seq(rg), lb_logits, nw_t)
    o_att = _attn_prompt(lam_params, seq(aq), akt, seq(av), subln, lam_init)
    hp, hn_p, route_p, hist_p = _mix(xp2, o_rec.reshape(tp, REC_WIDTH), o_att.reshape(tp, ATT_WIDTH), gates,
                                     wur, wua, wo, ln2, wr, br)

    xs2 = x_sample.reshape(bs, d)
    rq_s, zf_s, rv_s, rg_s, aq_s, ak_s, av_s, gates_s = _in_proj(xs2, ln1, w_in_b, seg_mean, qn_t, kn_t)
    tok = lambda a: a.reshape(bs, 1, a.shape[-1])
    o_rec_s, s_s = _rec_step(tok(rq_s), tok(zf_s), tok(rv_s), tok(rg_s), state_rec[layer].astype(F32),
                             lb_logits, nw_t)
    n_pool, page = cache_k.shape[1], cache_k.shape[2]
    ckt = jnp.transpose(cache_k[layer].reshape(n_pool, page, ATT_WIDTH), (0, 2, 1))
    cvf = cache_v[layer].reshape(n_pool, page * ATT_HEADS, ATT_V_DIM)
    q_bcast = jnp.broadcast_to(aq_s.astype(F32)[:, :, None], (bs, ATT_WIDTH, LANES))
    kn_bcast = jnp.broadcast_to(ak_s[:, :, None], (bs, ATT_WIDTH, LANES))
    o_att_s = _attn_sample(lam_params, q_bcast, kn_bcast, tok(av_s), subln, ckt, cvf,
                           page_table.astype(jnp.int32), lam_init)
    hs, hn_s, route_s, hist_s = _mix(xs2, o_rec_s.reshape(bs, REC_WIDTH), o_att_s.reshape(bs, ATT_WIDTH),
                                     gates_s, wur, wua, wo, ln2, wr, br)

    y_prompt = _combine(hp, _moe(hn_p, route_p, hist_p, wu, bu, wd, bd)).reshape(bp, lp_, d)
    y_sample = _combine(hs, _moe(hn_s, route_s, hist_s, wu, bu, wd, bd)).reshape(bs, 1, d)

    new_k_prompt = jnp.transpose(akt.reshape(bp, ATT_HEADS, 2, ATT_HEAD_DIM, lp_), (0, 4, 1, 2, 3))[None]
    new_v_prompt = av.reshape(1, bp, lp_, ATT_HEADS, ATT_V_DIM)
    new_rec_prompt = s_p.reshape(1, bp, REC_HEADS, REC_HEAD_DIM, REC_HEAD_DIM).astype(state_rec.dtype)
    new_k_sample = ak_s.reshape(1, bs, 1, ATT_HEADS, 2, ATT_HEAD_DIM)
    new_v_sample = av_s.reshape(1, bs, 1, ATT_HEADS, ATT_V_DIM)
    new_rec_sample = s_s.reshape(1, bs, REC_HEADS, REC_HEAD_DIM, REC_HEAD_DIM).astype(state_rec.dtype)
    return (y_prompt, y_sample, new_k_prompt, new_v_prompt, new_rec_prompt,
            new_k_sample, new_v_sample, new_rec_sample)
```

```python
import functools
import math

import numpy as np
import jax
import jax.numpy as jnp
from jax import lax
from jax.experimental import pallas as pl
from jax.experimental.pallas import tpu as pltpu

F32 = jnp.float32
BF16 = jnp.bfloat16

D_MODEL = 1024
REC_WIDTH = 512
REC_HEAD_DIM = 128
REC_HEADS = 4
ATT_WIDTH = 512
ATT_HEAD_DIM = 64
ATT_V_DIM = 128
ATT_HEADS = 4
N_EXPERTS = 32
TOP_K = 4
D_FF = 1024
SWIGLU_LIMIT = 7.0
SWIGLU_ALPHA = 1.702
CHUNK = 64
EPS = 1e-6
N_IN = 4 * REC_WIDTH + 3 * ATT_WIDTH + 2 * D_MODEL
LANES = 128
SUBLANES = 8
MXU_DIM = 256
VMEM_LIMIT = 48 * 1024 * 1024
LOG2E = math.log2(math.e)
Q_SCALE = (ATT_HEAD_DIM ** -0.5) * LOG2E

TOKEN_TILE = 512
GLA_BLOCK = 512
PAGE_SLOTS = 4
ATT_TILE = 512
ATT_ROWS = 256
MOE_TILE = 512
PAGE_GROUP = 8


def _sigmoid(x):
    return 1.0 / (1.0 + jnp.exp(-x))


def _dot(a, b):
    return jnp.dot(a, b, preferred_element_type=F32)


def _dot_nt(a, b):
    return lax.dot_general(a, b, (((1,), (1,)), ((), ())), preferred_element_type=F32)


def _dot_tn(a, b):
    return lax.dot_general(a, b, (((0,), (0,)), ((), ())), preferred_element_type=F32)


def _split_dot(x, m_bf16, terms):
    acc = None
    r = x
    for t in range(terms):
        p = r.astype(BF16)
        d = _dot(p, m_bf16)
        acc = d if acc is None else acc + d
        if t + 1 < terms:
            r = r - p.astype(F32)
    return acc


def _lambda_init(layer):
    return 0.8 - 0.6 * math.exp(-0.3 * layer)


def _lam_from_params(lp, lam_init):
    a = jnp.sum(lp[0:1, :] * lp[1:2, :], axis=-1, keepdims=True)
    b = jnp.sum(lp[2:3, :] * lp[3:4, :], axis=-1, keepdims=True)
    return jnp.exp(a) - jnp.exp(b) + lam_init


def _lower_bound(lbl):
    a0 = lbl[0:1, :]
    a1 = lbl[1:2, :]
    mx = jnp.maximum(a0, a1)
    e0 = jnp.exp(a0 - mx)
    e1 = jnp.exp(a1 - mx)
    return e0 / (e0 + e1)


def _in_proj_kernel(k_feature_major, x_ref, ln_ref, w_ref, seg_ref, qn_ref, kn_ref,
                    rq_ref, zf_ref, rv_ref, rg_ref, aq_ref, ak_ref, av_ref, gate_ref):
    x = x_ref[...]
    ms = jnp.mean(x * x, axis=-1, keepdims=True)
    xn = (x * lax.rsqrt(ms + EPS) * ln_ref[...]).astype(BF16)

    def proj(lo, hi):
        return _dot(xn, w_ref[:, lo:hi])

    def seg_norm(a, w):
        ms_seg = _split_dot(a * a, seg_ref[...], 2)
        return a * lax.rsqrt(ms_seg + EPS) * w

    r = REC_WIDTH
    rq_ref[...] = proj(0, r).astype(BF16)
    zf_ref[...] = proj(r, 2 * r)
    rv_ref[...] = proj(2 * r, 3 * r).astype(BF16)
    rg_ref[...] = proj(3 * r, 4 * r).astype(BF16)
    a0 = 4 * r
    aq = seg_norm(proj(a0, a0 + ATT_WIDTH), qn_ref[...])
    aq_ref[...] = (aq * Q_SCALE).astype(BF16)
    ak = seg_norm(proj(a0 + ATT_WIDTH, a0 + 2 * ATT_WIDTH), kn_ref[...])
    ak_ref[...] = ak.T if k_feature_major else ak
    av = proj(a0 + 2 * ATT_WIDTH, a0 + 3 * ATT_WIDTH)
    rows = av.shape[0]
    for h in range(ATT_HEADS):
        av_ref[pl.ds(h, rows, stride=ATT_HEADS), :] = av[:, h * ATT_V_DIM:(h + 1) * ATT_V_DIM]
    g0 = a0 + 3 * ATT_WIDTH
    gate_ref[...] = _sigmoid(proj(g0, g0 + 2 * D_MODEL)).astype(BF16)


def _in_proj(x2d, ln_w, w_in_bf16, seg_mat, qn_t, kn_t, seq_len=None):
    t = x2d.shape[0]
    tm = min(TOKEN_TILE, t)
    assert t % tm == 0
    row = lambda w: pl.BlockSpec((tm, w), lambda i: (i, 0))
    full = lambda a: pl.BlockSpec(a.shape, lambda i: (0,) * a.ndim)
    if seq_len is None:
        k_shape = jax.ShapeDtypeStruct((t, ATT_WIDTH), F32)
        k_spec = row(ATT_WIDTH)
    else:
        assert seq_len % tm == 0 and t % seq_len == 0
        per_seq = seq_len // tm
        k_shape = jax.ShapeDtypeStruct((t // seq_len, ATT_WIDTH, seq_len), F32)
        k_spec = pl.BlockSpec((None, ATT_WIDTH, tm), lambda i: (i // per_seq, 0, i % per_seq))
    out_shapes = (
        jax.ShapeDtypeStruct((t, REC_WIDTH), BF16),
        jax.ShapeDtypeStruct((t, REC_WIDTH), F32),
        jax.ShapeDtypeStruct((t, REC_WIDTH), BF16),
        jax.ShapeDtypeStruct((t, REC_WIDTH), BF16),
        jax.ShapeDtypeStruct((t, ATT_WIDTH), BF16),
        k_shape,
        jax.ShapeDtypeStruct((t * ATT_HEADS, ATT_V_DIM), F32),
        jax.ShapeDtypeStruct((t, 2 * D_MODEL), BF16),
    )
    v_spec = pl.BlockSpec((tm * ATT_HEADS, ATT_V_DIM), lambda i: (i, 0))
    out_specs = tuple(k_spec if s is k_shape else row(s.shape[1]) for s in out_shapes)
    out_specs = out_specs[:6] + (v_spec,) + out_specs[7:]
    return pl.pallas_call(
        functools.partial(_in_proj_kernel, seq_len is not None),
        grid=(t // tm,),
        in_specs=[row(D_MODEL), full(ln_w), full(w_in_bf16), full(seg_mat), full(qn_t), full(kn_t)],
        out_specs=out_specs,
        out_shape=out_shapes,
        compiler_params=pltpu.CompilerParams(dimension_semantics=("arbitrary",),
                                             vmem_limit_bytes=VMEM_LIMIT),
        name="in_proj",
    )(x2d, ln_w, w_in_bf16, seg_mat, qn_t, kn_t)


def _rec_out(o, nw, g):
    ms = jnp.mean(o * o, axis=-1, keepdims=True)
    return o * lax.rsqrt(ms + EPS) * nw * (g * _sigmoid(g))


def _split_dot_left(m_bf16, x):
    p1 = x.astype(BF16)
    r1 = x - p1.astype(F32)
    p2 = r1.astype(BF16)
    p3 = (r1 - p2.astype(F32)).astype(BF16)
    return _dot(m_bf16, p1) + _dot(m_bf16, p2) + _dot(m_bf16, p3)


def _gla_kernel(q_ref, zf_ref, v_ref, g_ref, lbl_ref, nw_ref, o_ref, sfin_ref, st_ref):
    j = pl.program_id(1)

    @pl.when(j == 0)
    def _():
        st_ref[...] = jnp.zeros_like(st_ref)

    lb = _lower_bound(lbl_ref[...])
    c = CHUNK
    ri = lax.broadcasted_iota(jnp.int32, (c, c), 0)
    ci = lax.broadcasted_iota(jnp.int32, (c, c), 1)
    causal = ri >= ci
    tri = jnp.where(causal, 1.0, 0.0).astype(BF16)
    nw = nw_ref[...]
    n_chunks = q_ref.shape[0] // c
    for n in range(n_chunks):
        rows = slice(n * c, (n + 1) * c)
        zf = zf_ref[rows, :]
        log_f = jnp.log(lb + (1.0 - lb) * _sigmoid(zf))
        k = (1.0 - lb) * _sigmoid(-zf)
        b = _split_dot_left(tri, log_f)
        b_last = b[c - 1:c, :]
        e_pos = jnp.exp(b)
        e_neg = jnp.exp(-b)
        e_tail = jnp.exp(b_last - b)
        decay = jnp.exp(b_last)
        q_e = (q_ref[rows, :].astype(F32) * e_pos).astype(BF16)
        k_e = (k * e_neg).astype(BF16)
        k_t = (k * e_tail).astype(BF16)
        v = v_ref[rows, :]
        g = g_ref[rows, :].astype(F32)
        for h in range(REC_HEADS):
            ln = slice(h * REC_HEAD_DIM, (h + 1) * REC_HEAD_DIM)
            s_t = st_ref[h]
            sc = jnp.where(causal, _dot_nt(q_e[:, ln], k_e[:, ln]), 0.0)
            o = _dot(sc.astype(BF16), v[:, ln]) + _dot_nt(q_e[:, ln], s_t.astype(BF16))
            st_ref[h] = decay[:, ln] * s_t + _dot_tn(v[:, ln], k_t[:, ln])
            o_ref[rows, ln] = _rec_out(o, nw[:, ln], g[:, ln]).astype(o_ref.dtype)

    @pl.when(j == pl.num_programs(1) - 1)
    def _():
        for h in range(REC_HEADS):
            sfin_ref[h] = st_ref[h].T


def _gla_prompt(rq, zf, rv, rg, lb_logits, nw_t):
    b, l, w = rq.shape
    assert l % CHUNK == 0
    lb_rows = min(GLA_BLOCK, l)
    assert l % lb_rows == 0
    seq = pl.BlockSpec((None, lb_rows, w), lambda i, j: (i, j, 0))
    full = lambda a: pl.BlockSpec(a.shape, lambda i, j: (0,) * a.ndim)
    return pl.pallas_call(
        _gla_kernel,
        grid=(b, l // lb_rows),
        in_specs=[seq, seq, seq, seq, full(lb_logits), full(nw_t)],
        out_specs=(seq, pl.BlockSpec((None, REC_HEADS, REC_HEAD_DIM, REC_HEAD_DIM),
                                     lambda i, j: (i, 0, 0, 0))),
        out_shape=(jax.ShapeDtypeStruct((b, l, w), BF16),
                   jax.ShapeDtypeStruct((b, REC_HEADS, REC_HEAD_DIM, REC_HEAD_DIM), F32)),
        scratch_shapes=[pltpu.VMEM((REC_HEADS, REC_HEAD_DIM, REC_HEAD_DIM), F32)],
        compiler_params=pltpu.CompilerParams(dimension_semantics=("arbitrary", "arbitrary"),
                                             vmem_limit_bytes=VMEM_LIMIT),
        name="gla_prompt",
    )(rq, zf, rv, rg, lb_logits, nw_t)


def _rec_step_kernel(q_ref, zf_ref, v_ref, g_ref, s0_ref, lbl_ref, nw_ref, o_ref, s_ref):
    lb = _lower_bound(lbl_ref[...])
    zf = zf_ref[...]
    f = lb + (1.0 - lb) * _sigmoid(zf)
    k = (1.0 - lb) * _sigmoid(-zf)
    q = q_ref[...].astype(F32)
    v = v_ref[...].astype(F32)
    g = g_ref[...].astype(F32)
    nw = nw_ref[...]
    d = REC_HEAD_DIM
    ri = lax.broadcasted_iota(jnp.int32, (d, d), 0)
    for h in range(REC_HEADS):
        ln = slice(h * d, (h + 1) * d)
        m = jnp.where(ri == 0, f[:, ln], jnp.where(ri == 1, k[:, ln], jnp.where(ri == 2, q[:, ln], 0.0)))
        mt = m.T
        s_new = mt[:, 0:1] * s0_ref[h] + mt[:, 1:2] * v[:, ln]
        s_ref[h] = s_new
        o = jnp.sum(mt[:, 2:3] * s_new, axis=0, keepdims=True)
        o_ref[:, ln] = _rec_out(o, nw[:, ln], g[:, ln]).astype(o_ref.dtype)


def _rec_step(rq, zf, rv, rg, s0, lb_logits, nw_t):
    b = rq.shape[0]
    w = rq.shape[-1]
    tok = pl.BlockSpec((None, 1, w), lambda i: (i, 0, 0))
    st = pl.BlockSpec((None, REC_HEADS, REC_HEAD_DIM, REC_HEAD_DIM), lambda i: (i, 0, 0, 0))
    full = lambda a: pl.BlockSpec(a.shape, lambda i: (0,) * a.ndim)
    return pl.pallas_call(
        _rec_step_kernel,
        grid=(b,),
        in_specs=[tok, tok, tok, tok, st, full(lb_logits), full(nw_t)],
        out_specs=(tok, st),
        out_shape=(jax.ShapeDtypeStruct((b, 1, w), BF16),
                   jax.ShapeDtypeStruct(s0.shape, F32)),
        compiler_params=pltpu.CompilerParams(dimension_semantics=("arbitrary",),
                                             vmem_limit_bytes=VMEM_LIMIT),
        name="rec_step",
    )(rq, zf, rv, rg, s0, lb_logits, nw_t)


def _subln(o, w, lam_init):
    ms = jnp.mean(o * o, axis=-1, keepdims=True)
    return o * lax.rsqrt(ms + EPS) * w * (1.0 - lam_init)


def _lane_block_max(x):
    blocks = [x[:, c * LANES:(c + 1) * LANES] for c in range(x.shape[1] // LANES)]
    return functools.reduce(jnp.maximum, blocks)


def _attn_prompt_kernel(lam_init, lp_ref, q_ref, kt_ref, v_ref, sw_ref, o_ref,
                        qs_ref, kb_ref, vb_ref, acc_ref):
    i = pl.program_id(2)
    tq = q_ref.shape[0]
    seq = kt_ref.shape[1]
    hd = ATT_V_DIM

    @pl.when(i == 0)
    def _():
        kb_ref[...] = kt_ref[...].astype(BF16)
        head = pl.program_id(1)
        vb_ref[:, 0:hd] = v_ref[pl.ds(head, seq, stride=ATT_HEADS), :].astype(BF16)
        vb_ref[:, hd:2 * hd] = jnp.ones((seq, hd), BF16)

    q = q_ref[...]
    lane = lax.broadcasted_iota(jnp.int32, q.shape, 1)
    first = lane < ATT_HEAD_DIM
    zero = jnp.zeros_like(q)
    qs_ref[0:tq, :] = jnp.where(first, q, zero)
    qs_ref[tq:2 * tq, :] = jnp.where(first, zero, q)
    neg = jnp.finfo(F32).min
    rows = min(ATT_ROWS, tq)
    n_chunks = 2 * tq // rows

    ri = lax.broadcasted_iota(jnp.int32, (rows, rows), 0)
    ci = lax.broadcasted_iota(jnp.int32, (rows, rows), 1)
    causal = ri >= ci

    def run(n_off):
        for r in range(n_chunks):
            row0 = r * rows
            n_full = n_off + row0 % tq
            qc = qs_ref[row0:row0 + rows, :]
            s_d = jnp.where(causal, _dot(qc, kb_ref[:, n_full:n_full + rows]), neg)
            mx = _lane_block_max(s_d)
            if n_full:
                s_o = _dot(qc, kb_ref[:, 0:n_full])
                mx = jnp.maximum(mx, _lane_block_max(s_o))
            m = jnp.max(mx, axis=-1, keepdims=True)
            acc = _dot(jnp.exp2(s_d - m).astype(BF16), vb_ref[n_full:n_full + rows, :])
            if n_full:
                acc = acc + _dot(jnp.exp2(s_o - m).astype(BF16), vb_ref[0:n_full, :])
            acc_ref[row0:row0 + rows, :] = acc

    for c in range(seq // tq):
        pl.when(i == c)(functools.partial(run, c * tq))

    lam = _lam_from_params(lp_ref[...], lam_init)
    a0 = acc_ref[0:tq, :]
    a1 = acc_ref[tq:2 * tq, :]
    o = a0[:, 0:hd] / a0[:, hd:2 * hd] - lam * (a1[:, 0:hd] / a1[:, hd:2 * hd])
    o_ref[...] = _subln(o, sw_ref[...], lam_init).astype(o_ref.dtype)


def _attn_prompt(lam_params, q, kt, v, subln_w, lam_init):
    b, l, w = q.shape
    tq = min(ATT_TILE, l)
    assert l % tq == 0
    hd = ATT_V_DIM
    qspec = pl.BlockSpec((None, tq, hd), lambda bi, h, i: (bi, i, h))
    ktspec = pl.BlockSpec((None, hd, l), lambda bi, h, i: (bi, h, 0))
    vspec = pl.BlockSpec((None, l * ATT_HEADS, hd), lambda bi, h, i: (bi, 0, 0))
    full = lambda a: pl.BlockSpec(a.shape, lambda bi, h, i: (0,) * a.ndim)
    return pl.pallas_call(
        functools.partial(_attn_prompt_kernel, lam_init),
        grid=(b, ATT_HEADS, l // tq),
        in_specs=[full(lam_params), qspec, ktspec, vspec, full(subln_w)],
        out_specs=qspec,
        out_shape=jax.ShapeDtypeStruct((b, l, w), BF16),
        scratch_shapes=[pltpu.VMEM((2 * tq, hd), BF16), pltpu.VMEM((hd, l), BF16),
                        pltpu.VMEM((l, 2 * hd), BF16), pltpu.VMEM((2 * tq, 2 * hd), F32)],
        compiler_params=pltpu.CompilerParams(
            dimension_semantics=("arbitrary", "arbitrary", "arbitrary"),
            vmem_limit_bytes=VMEM_LIMIT),
        name="attn_prompt",
    )(lam_params, q, kt, v, subln_w)


def _seg_scores(feat_by_tok, q_bcast):
    n = feat_by_tok.shape[1]
    prod = feat_by_tok * q_bcast
    return jnp.sum(prod.reshape(2 * ATT_HEADS, ATT_HEAD_DIM, n), axis=1)


def _attn_sample_kernel(lam_init, pt_ref, lp_ref, qb_ref, knb_ref, vn_ref, sw_ref, ck_ref, cv_ref,
                        o_ref, kbuf, vbuf, sem_k, sem_v):
    b = pl.program_id(0)
    nb = pl.num_programs(0)
    n_pages = pt_ref.shape[1]
    slots = kbuf.shape[0]
    ahead = slots - 1
    grp = kbuf.shape[1]
    n_groups = n_pages // grp
    page = kbuf.shape[3]
    total = nb * n_groups

    def copies(n):
        bi = n // n_groups
        gi = n % n_groups
        slot = n % slots
        out = []
        for j in range(grp):
            pg = pt_ref[bi, gi * grp + j]
            out.append(pltpu.make_async_copy(ck_ref.at[pg], kbuf.at[slot, j], sem_k.at[slot]))
            out.append(pltpu.make_async_copy(cv_ref.at[pg], vbuf.at[slot, j], sem_v.at[slot]))
        return out

    @pl.when(b == 0)
    def _():
        for n in range(ahead):
            for cp in copies(n):
                cp.start()

    qb = qb_ref[...]
    m0 = _seg_scores(knb_ref[...], qb)[:, 0:1]
    l0 = jnp.ones_like(m0)
    acc0 = tuple(jnp.broadcast_to(vn_ref[:, h * ATT_V_DIM:(h + 1) * ATT_V_DIM], (2 * ATT_HEADS, ATT_V_DIM))
                 for h in range(ATT_HEADS))

    def body(gi, carry):
        m, l, acc = carry
        n = b * n_groups + gi
        slot = n % slots

        @pl.when(n + ahead < total)
        def _():
            for cp in copies(n + ahead):
                cp.start()

        for cp in copies(n):
            cp.wait()
        s = jnp.concatenate([_seg_scores(kbuf[slot, j], qb) for j in range(grp)], axis=1)
        m_new = jnp.maximum(m, jnp.max(s, axis=-1, keepdims=True))
        p = jnp.exp2(s - m_new)
        alpha = jnp.exp2(m - m_new)
        l_new = alpha * l + jnp.sum(p, axis=-1, keepdims=True)
        pb = p.astype(BF16)
        new_acc = []
        for h in range(ATT_HEADS):
            a = alpha * acc[h]
            for j in range(grp):
                vh = vbuf[slot, j, pl.ds(h, page, stride=ATT_HEADS), :].astype(BF16)
                a = a + _dot(pb[:, j * page:(j + 1) * page], vh)
            new_acc.append(a)
        return m_new, l_new, tuple(new_acc)

    m, l, acc = lax.fori_loop(0, n_groups, body, (m0, l0, acc0))
    lam = _lam_from_params(lp_ref[...], lam_init)
    sw = sw_ref[...]
    for h in range(ATT_HEADS):
        r0, r1 = 2 * h, 2 * h + 1
        o = acc[h][r0:r0 + 1, :] / l[r0:r0 + 1, :] - lam * (acc[h][r1:r1 + 1, :] / l[r1:r1 + 1, :])
        o_ref[:, h * ATT_V_DIM:(h + 1) * ATT_V_DIM] = _subln(o, sw, lam_init).astype(o_ref.dtype)


def _attn_sample(lam_params, q_bcast, knew_bcast, v_new, subln_w, cache_kt, cache_vf, page_table, lam_init):
    b = q_bcast.shape[0]
    n_pages = page_table.shape[1]
    page = cache_kt.shape[2]
    grp = min(PAGE_GROUP, n_pages)
    slots = PAGE_SLOTS
    assert n_pages % grp == 0 and b * (n_pages // grp) >= slots - 1
    w = ATT_WIDTH
    bc = pl.BlockSpec((None, w, LANES), lambda bi, pt: (bi, 0, 0))
    tok = pl.BlockSpec((None, 1, w), lambda bi, pt: (bi, 0, 0))
    full = lambda a: pl.BlockSpec(a.shape, lambda bi, pt: (0,) * a.ndim)
    hbm = pl.BlockSpec(memory_space=pl.ANY)
    grid_spec = pltpu.PrefetchScalarGridSpec(
        num_scalar_prefetch=1,
        grid=(b,),
        in_specs=[full(lam_params), bc, bc, tok, full(subln_w), hbm, hbm],
        out_specs=tok,
        scratch_shapes=[pltpu.VMEM((slots, grp, w, page), F32),
                        pltpu.VMEM((slots, grp, page * ATT_HEADS, ATT_V_DIM), F32),
                        pltpu.SemaphoreType.DMA((slots,)), pltpu.SemaphoreType.DMA((slots,))],
    )
    return pl.pallas_call(
        functools.partial(_attn_sample_kernel, lam_init),
        grid_spec=grid_spec,
        out_shape=jax.ShapeDtypeStruct((b, 1, w), BF16),
        compiler_params=pltpu.CompilerParams(dimension_semantics=("arbitrary",),
                                             vmem_limit_bytes=VMEM_LIMIT),
        name="attn_sample",
    )(page_table, lam_params, q_bcast, knew_bcast, v_new, subln_w, cache_kt, cache_vf)


def _mix_kernel(x_ref, orec_ref, oatt_ref, gate_ref, wur_ref, wua_ref, wo_ref, ln2_ref, wr_ref, br_ref,
                hp_ref, hn_ref, route_ref, hist_ref):
    y_rec = _dot(orec_ref[...], wur_ref[...])
    y_att = _dot(oatt_ref[...], wua_ref[...])
    g_rec = gate_ref[:, :D_MODEL].astype(F32)
    g_att = gate_ref[:, D_MODEL:].astype(F32)
    mix = (g_rec * y_rec + g_att * y_att).astype(BF16)
    hp = x_ref[...] + _dot(mix, wo_ref[...])
    hp_ref[...] = hp
    ms = jnp.mean(hp * hp, axis=-1, keepdims=True)
    hn = (hp * lax.rsqrt(ms + EPS) * ln2_ref[...]).astype(BF16)
    hn_ref[...] = hn
    logits = _dot(hn, wr_ref[...]) + br_ref[...]
    lt = logits.T[0:N_EXPERTS, :]
    row = lax.broadcasted_iota(jnp.int32, lt.shape, 0).astype(F32)
    vals, idxs = [], []
    for _ in range(TOP_K):
        mk = jnp.max(lt, axis=0, keepdims=True)
        ik = jnp.min(jnp.where(lt == mk, row, float(N_EXPERTS)), axis=0, keepdims=True)
        vals.append(mk)
        idxs.append(ik)
        lt = jnp.where(row == ik, -jnp.inf, lt)
    es = [jnp.exp(v - vals[0]) for v in vals]
    denom = es[0] + es[1] + es[2] + es[3]
    route_ref[...] = jnp.concatenate(idxs + [e / denom for e in es], axis=0)
    hits = jnp.where(row == idxs[0], 1.0, 0.0)
    for kk in range(1, TOP_K):
        hits = hits + jnp.where(row == idxs[kk], 1.0, 0.0)
    hist_ref[...] = jnp.sum(hits, axis=1, keepdims=True)


def _mix(x2d, orec, oatt, gates, wur, wua, wo, ln2, wr, br):
    t = x2d.shape[0]
    tm = min(TOKEN_TILE, t)
    assert t % tm == 0
    row = lambda w: pl.BlockSpec((tm, w), lambda i: (i, 0))
    full = lambda a: pl.BlockSpec(a.shape, lambda i: (0,) * a.ndim)
    return pl.pallas_call(
        _mix_kernel,
        grid=(t // tm,),
        in_specs=[row(D_MODEL), row(REC_WIDTH), row(ATT_WIDTH), row(2 * D_MODEL),
                  full(wur), full(wua), full(wo), full(ln2), full(wr), full(br)],
        out_specs=(row(D_MODEL), row(D_MODEL),
                   pl.BlockSpec((2 * TOP_K, tm), lambda i: (0, i)),
                   pl.BlockSpec((None, N_EXPERTS, 1), lambda i: (i, 0, 0))),
        out_shape=(jax.ShapeDtypeStruct((t, D_MODEL), F32),
                   jax.ShapeDtypeStruct((t, D_MODEL), BF16),
                   jax.ShapeDtypeStruct((2 * TOP_K, t), F32),
                   jax.ShapeDtypeStruct((t // tm, N_EXPERTS, 1), F32)),
        compiler_params=pltpu.CompilerParams(dimension_semantics=("arbitrary",),
                                             vmem_limit_bytes=VMEM_LIMIT),
        name="mix_route",
    )(x2d, orec, oatt, gates, wur, wua, wo, ln2, wr, br)


def _pair_split_matrix():
    m = np.zeros((MXU_DIM, MXU_DIM), np.float32)
    half = MXU_DIM // 2
    m[2 * np.arange(half), np.arange(half)] = 1.0
    m[2 * np.arange(half) + 1, half + np.arange(half)] = 1.0
    return m


def _wprep_kernel(w_ref, p_ref, o_ref):
    for g in range(w_ref.shape[1] // MXU_DIM):
        cols = slice(g * MXU_DIM, (g + 1) * MXU_DIM)
        o_ref[:, cols] = _dot(w_ref[:, cols].astype(BF16), p_ref[...]).astype(o_ref.dtype)


def _prep_up_weights(w_up, split_mat):
    e, d, f2 = w_up.shape
    return pl.pallas_call(
        _wprep_kernel,
        grid=(e,),
        in_specs=[pl.BlockSpec((None, d, f2), lambda i: (i, 0, 0)),
                  pl.BlockSpec(split_mat.shape, lambda i: (0, 0))],
        out_specs=pl.BlockSpec((None, d, f2), lambda i: (i, 0, 0)),
        out_shape=jax.ShapeDtypeStruct((e, d, f2), BF16),
        compiler_params=pltpu.CompilerParams(dimension_semantics=("arbitrary",),
                                             vmem_limit_bytes=VMEM_LIMIT),
        name="moe_wprep",
    )(w_up, split_mat)


def _moe_kernel(wt_ref, we_ref, lo_ref, hi_ref, first_ref, x_ref, wu_ref, bu_ref, wd_ref, bd_ref, rw_ref,
                y_ref):
    del we_ref
    w = pl.program_id(0)
    lo = lo_ref[w]
    hi = hi_ref[w]

    tm = y_ref.shape[0]

    @pl.when(first_ref[w] == 1)
    def _():
        y_ref[...] = jnp.zeros_like(y_ref)

    @pl.when(hi > lo)
    def _():
        rw_col = jnp.broadcast_to(rw_ref[...], (LANES, tm)).T[:, 0:1]
        row = lax.broadcasted_iota(jnp.int32, (tm, 1), 0) + wt_ref[w] * tm
        scale = jnp.where((row >= lo) & (row < hi), rw_col, 0.0)
        half = MXU_DIM // 2
        n_grp = 2 * D_FF // MXU_DIM
        n_parts = 2 if tm % (2 * SUBLANES * 2) == 0 else 1
        part = tm // n_parts
        for r in range(n_parts):
            rows = slice(r * part, (r + 1) * part)
            h = _dot(x_ref[rows, :], wu_ref[...]) + bu_ref[...]
            glu = jnp.concatenate([h[:, g * MXU_DIM:g * MXU_DIM + half] for g in range(n_grp)], axis=1)
            lin = jnp.concatenate([h[:, g * MXU_DIM + half:(g + 1) * MXU_DIM] for g in range(n_grp)], axis=1)
            glu = jnp.minimum(glu, SWIGLU_LIMIT)
            lin = jnp.clip(lin, -SWIGLU_LIMIT, SWIGLU_LIMIT)
            a = glu * _sigmoid(SWIGLU_ALPHA * glu) * (lin + 1.0)
            y = scale[rows, :] * (_dot(a.astype(BF16), wd_ref[...]) + bd_ref[...])
            y_ref[rows, :] = y_ref[rows, :] + y.astype(y_ref.dtype)


def _moe_ffn(work, x_sorted, wu, bu, wd, bd, row_w, tm):
    n_rows = x_sorted.shape[0]
    assert n_rows % tm == 0
    row = lambda c: pl.BlockSpec((tm, c), lambda w, wt, we, lo, hi, fi: (wt[w], 0))
    per_e = lambda r, c: pl.BlockSpec((None, r, c), lambda w, wt, we, lo, hi, fi: (we[w], 0, 0))
    grid_spec = pltpu.PrefetchScalarGridSpec(
        num_scalar_prefetch=5,
        grid=(work[0].shape[0],),
        in_specs=[row(D_MODEL), per_e(D_MODEL, 2 * D_FF), per_e(1, 2 * D_FF),
                  per_e(D_FF, D_MODEL), per_e(1, D_MODEL),
                  pl.BlockSpec((None, 1, tm), lambda w, wt, we, lo, hi, fi: (wt[w], 0, 0))],
        out_specs=row(D_MODEL),
    )
    return pl.pallas_call(
        _moe_kernel,
        grid_spec=grid_spec,
        out_shape=jax.ShapeDtypeStruct((n_rows, D_MODEL), BF16),
        compiler_params=pltpu.CompilerParams(dimension_semantics=("arbitrary",),
                                             vmem_limit_bytes=VMEM_LIMIT),
        name="moe_ffn",
    )(*work, x_sorted, wu, bu, wd, bd, row_w)


def _combine_kernel(hp_ref, y_ref, o_ref):
    moe = y_ref[0].astype(F32)
    for kk in range(1, TOP_K):
        moe = moe + y_ref[kk].astype(F32)
    o_ref[...] = hp_ref[...] + moe


def _combine(hp, y_rows):
    t = hp.shape[0]
    tm = min(TOKEN_TILE, t)
    assert t % tm == 0
    return pl.pallas_call(
        _combine_kernel,
        grid=(t // tm,),
        in_specs=[pl.BlockSpec((tm, D_MODEL), lambda i: (i, 0)),
                  pl.BlockSpec((TOP_K, tm, D_MODEL), lambda i: (0, i, 0))],
        out_specs=pl.BlockSpec((tm, D_MODEL), lambda i: (i, 0)),
        out_shape=jax.ShapeDtypeStruct((t, D_MODEL), F32),
        compiler_params=pltpu.CompilerParams(dimension_semantics=("arbitrary",),
                                             vmem_limit_bytes=VMEM_LIMIT),
        name="moe_combine",
    )(hp, y_rows)


def _moe(hn, route, hist, wu, bu, wd, bd):
    t = hn.shape[0]
    n = t * TOP_K
    tm = min(MOE_TILE, t)
    assert n % tm == 0
    n_tiles = n // tm
    shift = max(1, (n - 1).bit_length())
    assert N_EXPERTS << shift < 2 ** 31
    e_flat = route[:TOP_K].astype(jnp.int32).reshape(n)
    w_flat = route[TOP_K:2 * TOP_K].reshape(n)
    ar = jnp.arange(n, dtype=jnp.int32)
    keys = lax.sort((e_flat << shift) | ar, is_stable=False)
    order = keys & ((1 << shift) - 1)
    _, pos = lax.sort((order, ar), num_keys=1, is_stable=False)
    counts = jnp.sum(hist, axis=(0, 2)).astype(jnp.int32)
    c_end = jnp.cumsum(counts)
    c_start = c_end - counts
    cuts = lax.sort(jnp.concatenate([jnp.arange(n_tiles, dtype=jnp.int32) * tm, c_start[1:]]), is_stable=False)
    lo = cuts
    hi = jnp.concatenate([cuts[1:], jnp.full((1,), n, jnp.int32)])
    w_tile = jnp.minimum(lo // tm, n_tiles - 1)
    w_expert = jnp.minimum(jnp.sum((c_end[None, :] <= lo[:, None]).astype(jnp.int32), axis=1), N_EXPERTS - 1)
    first = ((lo == w_tile * tm) & (hi > lo)).astype(jnp.int32)
    token = (order & (t - 1)) if t & (t - 1) == 0 else order % t
    x_sorted = hn.at[token].get(mode="promise_in_bounds")
    row_w = w_flat.at[order].get(mode="promise_in_bounds").reshape(n_tiles, 1, tm)
    y_sorted = _moe_ffn((w_tile, w_expert, lo, hi, first), x_sorted, wu, bu, wd, bd, row_w, tm)
    return y_sorted.at[pos].get(mode="promise_in_bounds").reshape(TOP_K, t, D_MODEL)


def _seg_mean_matrix():
    g = np.arange(ATT_WIDTH) // ATT_HEAD_DIM
    return jnp.asarray((g[:, None] == g[None, :]).astype(np.float32) / ATT_HEAD_DIM, dtype=BF16)


def kernel(x_prompt, x_sample, cache_k, cache_v, state_rec, page_table, ln1_w, w_in, rec_lb_logits,
           rec_norm_w, w_up_rec, q_norm_w, k_norm_w, lambda_q1, lambda_k1, lambda_q2, lambda_k2,
           att_subln_w, w_up_att, w_out, ln2_w, w_router, b_router, w_exp_up, b_exp_up,
           w_exp_down, b_exp_down):
    depth = ln1_w.shape[0]
    assert depth == 1 and x_sample.shape[1] == 1
    layer = 0
    bp, lp_, d = x_prompt.shape
    bs = x_sample.shape[0]
    tp = bp * lp_
    lam_init = _lambda_init(layer)

    w_in_b = w_in[layer].astype(BF16)
    ln1 = ln1_w[layer].reshape(1, d)
    ln2 = ln2_w[layer].reshape(1, d)
    qn_t = jnp.tile(q_norm_w[layer], ATT_WIDTH // ATT_HEAD_DIM).reshape(1, ATT_WIDTH)
    kn_t = jnp.tile(k_norm_w[layer], ATT_WIDTH // ATT_HEAD_DIM).reshape(1, ATT_WIDTH)
    nw_t = jnp.tile(rec_norm_w[layer], REC_HEADS).reshape(1, REC_WIDTH)
    subln = att_subln_w[layer].reshape(1, ATT_V_DIM)
    lam_params = jnp.stack([lambda_q1[layer], lambda_k1[layer], lambda_q2[layer], lambda_k2[layer]]).astype(F32)
    lb_logits = rec_lb_logits.astype(F32)[layer:layer + 2]
    wur = w_up_rec[layer].astype(BF16)
    wua = w_up_att[layer].astype(BF16)
    wo = w_out[layer].astype(BF16)
    wr = jnp.zeros((d, LANES), BF16).at[:, :N_EXPERTS].set(w_router[layer].astype(BF16))
    br = jnp.full((1, LANES), -1e30, F32).at[0, :N_EXPERTS].set(b_router[layer].astype(F32))
    wu = _prep_up_weights(w_exp_up[layer], jnp.asarray(_pair_split_matrix(), dtype=BF16))
    half = MXU_DIM // 2
    bu = (b_exp_up[layer].reshape(N_EXPERTS, 2 * D_FF // MXU_DIM, half, 2)
          .transpose(0, 1, 3, 2).reshape(N_EXPERTS, 1, 2 * D_FF))
    wd = w_exp_down[layer].astype(BF16)
    bd = b_exp_down[layer].reshape(N_EXPERTS, 1, D_MODEL)
    seg_mean = _seg_mean_matrix()

    xp2 = x_prompt.reshape(tp, d)
    rq, zf, rv, rg, aq, akt, av, gates = _in_proj(xp2, ln1, w_in_b, seg_mean, qn_t, kn_t, seq_len=lp_)
    seq = lambda a: a.reshape(bp, lp_, a.shape[-1])
    o_rec, s_p = _gla_prompt(seq(rq), seq(zf), seq(rv), seq(rg), lb_logits, nw_t)
    o_att = _attn_prompt(lam_params, seq(aq), akt, av.reshape(bp, lp_ * ATT_HEADS, ATT_V_DIM), subln, lam_init)
    hp, hn_p, route_p, hist_p = _mix(xp2, o_rec.reshape(tp, REC_WIDTH), o_att.reshape(tp, ATT_WIDTH), gates,
                                     wur, wua, wo, ln2, wr, br)

    xs2 = x_sample.reshape(bs, d)
    rq_s, zf_s, rv_s, rg_s, aq_s, ak_s, av_s, gates_s = _in_proj(xs2, ln1, w_in_b, seg_mean, qn_t, kn_t)
    tok = lambda a: a.reshape(bs, 1, a.shape[-1])
    o_rec_s, s_s = _rec_step(tok(rq_s), tok(zf_s), tok(rv_s), tok(rg_s), state_rec[layer].astype(F32),
                             lb_logits, nw_t)
    n_pool, page = cache_k.shape[1], cache_k.shape[2]
    ckt = jnp.transpose(cache_k[layer].reshape(n_pool, page, ATT_WIDTH), (0, 2, 1))
    cvf = cache_v[layer].reshape(n_pool, page * ATT_HEADS, ATT_V_DIM)
    q_bcast = jnp.broadcast_to(aq_s.astype(F32)[:, :, None], (bs, ATT_WIDTH, LANES))
    kn_bcast = jnp.broadcast_to(ak_s[:, :, None], (bs, ATT_WIDTH, LANES))
    o_att_s = _attn_sample(lam_params, q_bcast, kn_bcast, av_s.reshape(bs, 1, ATT_WIDTH), subln, ckt, cvf,
                           page_table.astype(jnp.int32), lam_init)
    hs, hn_s, route_s, hist_s = _mix(xs2, o_rec_s.reshape(bs, REC_WIDTH), o_att_s.reshape(bs, ATT_WIDTH),
                                     gates_s, wur, wua, wo, ln2, wr, br)

    y_prompt = _combine(hp, _moe(hn_p, route_p, hist_p, wu, bu, wd, bd)).reshape(bp, lp_, d)
    y_sample = _combine(hs, _moe(hn_s, route_s, hist_s, wu, bu, wd, bd)).reshape(bs, 1, d)

    new_k_prompt = jnp.transpose(akt.reshape(bp, ATT_HEADS, 2, ATT_HEAD_DIM, lp_), (0, 4, 1, 2, 3))[None]
    new_v_prompt = av.reshape(1, bp, lp_, ATT_HEADS, ATT_V_DIM)
    new_rec_prompt = s_p.reshape(1, bp, REC_HEADS, REC_HEAD_DIM, REC_HEAD_DIM).astype(state_rec.dtype)
    new_k_sample = ak_s.reshape(1, bs, 1, ATT_HEADS, 2, ATT_HEAD_DIM)
    new_v_sample = av_s.reshape(1, bs, 1, ATT_HEADS, ATT_V_DIM)
    new_rec_sample = s_s.reshape(1, bs, REC_HEADS, REC_HEAD_DIM, REC_HEAD_DIM).astype(state_rec.dtype)
    return (y_prompt, y_sample, new_k_prompt, new_v_prompt, new_rec_prompt,
            new_k_sample, new_v_sample, new_rec_sample)
```

```python
import functools
import math

import numpy as np
import jax
import jax.numpy as jnp
from jax import lax
from jax.experimental import pallas as pl
from jax.experimental.pallas import tpu as pltpu

F32 = jnp.float32
BF16 = jnp.bfloat16

D_MODEL = 1024
REC_WIDTH = 512
REC_HEAD_DIM = 128
REC_HEADS = 4
ATT_WIDTH = 512
ATT_HEAD_DIM = 64
ATT_V_DIM = 128
ATT_HEADS = 4
N_EXPERTS = 32
TOP_K = 4
D_FF = 1024
SWIGLU_LIMIT = 7.0
SWIGLU_ALPHA = 1.702
CHUNK = 64
EPS = 1e-6
N_IN = 4 * REC_WIDTH + 3 * ATT_WIDTH + 2 * D_MODEL
LANES = 128
SUBLANES = 8
MXU_DIM = 256
VMEM_LIMIT = 48 * 1024 * 1024
LOG2E = math.log2(math.e)
Q_SCALE = (ATT_HEAD_DIM ** -0.5) * LOG2E

TOKEN_TILE = 512
GLA_BLOCK = 512
PAGE_SLOTS = 3
ATT_TILE = 512
ATT_ROWS = 256
MOE_TILE = 512
PAGE_GROUP = 8


def _sigmoid(x):
    return 1.0 / (1.0 + jnp.exp(-x))


def _dot(a, b):
    return jnp.dot(a, b, preferred_element_type=F32)


def _dot_nt(a, b):
    return lax.dot_general(a, b, (((1,), (1,)), ((), ())), preferred_element_type=F32)


def _dot_tn(a, b):
    return lax.dot_general(a, b, (((0,), (0,)), ((), ())), preferred_element_type=F32)


def _split_dot(x, m_bf16, terms):
    acc = None
    r = x
    for t in range(terms):
        p = r.astype(BF16)
        d = _dot(p, m_bf16)
        acc = d if acc is None else acc + d
        if t + 1 < terms:
            r = r - p.astype(F32)
    return acc


def _lambda_init(layer):
    return 0.8 - 0.6 * math.exp(-0.3 * layer)


def _lam_from_params(lp, lam_init):
    a = jnp.sum(lp[0:1, :] * lp[1:2, :], axis=-1, keepdims=True)
    b = jnp.sum(lp[2:3, :] * lp[3:4, :], axis=-1, keepdims=True)
    return jnp.exp(a) - jnp.exp(b) + lam_init


def _lower_bound(lbl):
    a0 = lbl[0:1, :]
    a1 = lbl[1:2, :]
    mx = jnp.maximum(a0, a1)
    e0 = jnp.exp(a0 - mx)
    e1 = jnp.exp(a1 - mx)
    return e0 / (e0 + e1)


def _in_proj_kernel(k_feature_major, x_ref, ln_ref, w_ref, seg_ref, qn_ref, kn_ref,
                    rq_ref, zf_ref, rv_ref, rg_ref, aq_ref, ak_ref, av_ref, gate_ref):
    x = x_ref[...]
    ms = jnp.mean(x * x, axis=-1, keepdims=True)
    xn = (x * lax.rsqrt(ms + EPS) * ln_ref[...]).astype(BF16)

    def proj(lo, hi):
        return _dot(xn, w_ref[:, lo:hi])

    def seg_norm(a, w):
        ms_seg = _split_dot(a * a, seg_ref[...], 2)
        return a * lax.rsqrt(ms_seg + EPS) * w

    r = REC_WIDTH
    rq_ref[...] = proj(0, r).astype(BF16)
    zf_ref[...] = proj(r, 2 * r)
    rv_ref[...] = proj(2 * r, 3 * r).astype(BF16)
    rg_ref[...] = proj(3 * r, 4 * r).astype(BF16)
    a0 = 4 * r
    aq = seg_norm(proj(a0, a0 + ATT_WIDTH), qn_ref[...])
    aq_ref[...] = (aq * Q_SCALE).astype(BF16)
    ak = seg_norm(proj(a0 + ATT_WIDTH, a0 + 2 * ATT_WIDTH), kn_ref[...])
    ak_ref[...] = ak.T if k_feature_major else ak
    av = proj(a0 + 2 * ATT_WIDTH, a0 + 3 * ATT_WIDTH)
    rows = av.shape[0]
    for h in range(ATT_HEADS):
        av_ref[pl.ds(h, rows, stride=ATT_HEADS), :] = av[:, h * ATT_V_DIM:(h + 1) * ATT_V_DIM]
    g0 = a0 + 3 * ATT_WIDTH
    gate_ref[...] = _sigmoid(proj(g0, g0 + 2 * D_MODEL)).astype(BF16)


def _in_proj(x2d, ln_w, w_in_bf16, seg_mat, qn_t, kn_t, seq_len=None):
    t = x2d.shape[0]
    tm = min(TOKEN_TILE, t)
    assert t % tm == 0
    row = lambda w: pl.BlockSpec((tm, w), lambda i: (i, 0))
    full = lambda a: pl.BlockSpec(a.shape, lambda i: (0,) * a.ndim)
    if seq_len is None:
        k_shape = jax.ShapeDtypeStruct((t, ATT_WIDTH), F32)
        k_spec = row(ATT_WIDTH)
    else:
        assert seq_len % tm == 0 and t % seq_len == 0
        per_seq = seq_len // tm
        k_shape = jax.ShapeDtypeStruct((t // seq_len, ATT_WIDTH, seq_len), F32)
        k_spec = pl.BlockSpec((None, ATT_WIDTH, tm), lambda i: (i // per_seq, 0, i % per_seq))
    out_shapes = (
        jax.ShapeDtypeStruct((t, REC_WIDTH), BF16),
        jax.ShapeDtypeStruct((t, REC_WIDTH), F32),
        jax.ShapeDtypeStruct((t, REC_WIDTH), BF16),
        jax.ShapeDtypeStruct((t, REC_WIDTH), BF16),
        jax.ShapeDtypeStruct((t, ATT_WIDTH), BF16),
        k_shape,
        jax.ShapeDtypeStruct((t * ATT_HEADS, ATT_V_DIM), F32),
        jax.ShapeDtypeStruct((t, 2 * D_MODEL), BF16),
    )
    v_spec = pl.BlockSpec((tm * ATT_HEADS, ATT_V_DIM), lambda i: (i, 0))
    out_specs = tuple(k_spec if s is k_shape else row(s.shape[1]) for s in out_shapes)
    out_specs = out_specs[:6] + (v_spec,) + out_specs[7:]
    return pl.pallas_call(
        functools.partial(_in_proj_kernel, seq_len is not None),
        grid=(t // tm,),
        in_specs=[row(D_MODEL), full(ln_w), full(w_in_bf16), full(seg_mat), full(qn_t), full(kn_t)],
        out_specs=out_specs,
        out_shape=out_shapes,
        compiler_params=pltpu.CompilerParams(dimension_semantics=("arbitrary",),
                                             vmem_limit_bytes=VMEM_LIMIT),
        name="in_proj",
    )(x2d, ln_w, w_in_bf16, seg_mat, qn_t, kn_t)


def _rec_out(o, nw, g):
    ms = jnp.mean(o * o, axis=-1, keepdims=True)
    return o * lax.rsqrt(ms + EPS) * nw * (g * _sigmoid(g))


def _cumsum_rows(x):
    n = x.shape[0]
    row = lax.broadcasted_iota(jnp.int32, x.shape, 0)
    shift = 1
    while shift < n:
        x = x + jnp.where(row >= shift, pltpu.roll(x, shift, axis=0), 0.0)
        shift *= 2
    return x


def _gla_kernel(q_ref, zf_ref, v_ref, g_ref, lbl_ref, nw_ref, o_ref, sfin_ref, st_ref):
    j = pl.program_id(1)

    @pl.when(j == 0)
    def _():
        st_ref[...] = jnp.zeros_like(st_ref)

    lb = _lower_bound(lbl_ref[...])
    c = CHUNK
    ri = lax.broadcasted_iota(jnp.int32, (c, c), 0)
    ci = lax.broadcasted_iota(jnp.int32, (c, c), 1)
    causal = ri >= ci
    nw = nw_ref[...]
    n_chunks = q_ref.shape[0] // c
    for n in range(n_chunks):
        rows = slice(n * c, (n + 1) * c)
        zf = zf_ref[rows, :]
        log_f = jnp.log(lb + (1.0 - lb) * _sigmoid(zf))
        k = (1.0 - lb) * _sigmoid(-zf)
        b = _cumsum_rows(log_f)
        b_last = b[c - 1:c, :]
        e_pos = jnp.exp(b)
        e_neg = jnp.exp(-b)
        e_tail = jnp.exp(b_last - b)
        decay = jnp.exp(b_last)
        q_e = (q_ref[rows, :].astype(F32) * e_pos).astype(BF16)
        k_e = (k * e_neg).astype(BF16)
        k_t = (k * e_tail).astype(BF16)
        v = v_ref[rows, :]
        g = g_ref[rows, :].astype(F32)
        for h in range(REC_HEADS):
            ln = slice(h * REC_HEAD_DIM, (h + 1) * REC_HEAD_DIM)
            s_t = st_ref[h]
            sc = jnp.where(causal, _dot_nt(q_e[:, ln], k_e[:, ln]), 0.0)
            o = _dot(sc.astype(BF16), v[:, ln]) + _dot_nt(q_e[:, ln], s_t.astype(BF16))
            st_ref[h] = decay[:, ln] * s_t + _dot_tn(v[:, ln], k_t[:, ln])
            o_ref[rows, ln] = _rec_out(o, nw[:, ln], g[:, ln]).astype(o_ref.dtype)

    @pl.when(j == pl.num_programs(1) - 1)
    def _():
        for h in range(REC_HEADS):
            sfin_ref[h] = st_ref[h].T


def _gla_prompt(rq, zf, rv, rg, lb_logits, nw_t):
    b, l, w = rq.shape
    assert l % CHUNK == 0
    lb_rows = min(GLA_BLOCK, l)
    assert l % lb_rows == 0
    seq = pl.BlockSpec((None, lb_rows, w), lambda i, j: (i, j, 0))
    full = lambda a: pl.BlockSpec(a.shape, lambda i, j: (0,) * a.ndim)
    return pl.pallas_call(
        _gla_kernel,
        grid=(b, l // lb_rows),
        in_specs=[seq, seq, seq, seq, full(lb_logits), full(nw_t)],
        out_specs=(seq, pl.BlockSpec((None, REC_HEADS, REC_HEAD_DIM, REC_HEAD_DIM),
                                     lambda i, j: (i, 0, 0, 0))),
        out_shape=(jax.ShapeDtypeStruct((b, l, w), BF16),
                   jax.ShapeDtypeStruct((b, REC_HEADS, REC_HEAD_DIM, REC_HEAD_DIM), F32)),
        scratch_shapes=[pltpu.VMEM((REC_HEADS, REC_HEAD_DIM, REC_HEAD_DIM), F32)],
        compiler_params=pltpu.CompilerParams(dimension_semantics=("arbitrary", "arbitrary"),
                                             vmem_limit_bytes=VMEM_LIMIT),
        name="gla_prompt",
    )(rq, zf, rv, rg, lb_logits, nw_t)


def _rec_step_kernel(q_ref, zf_ref, v_ref, g_ref, s0_ref, lbl_ref, nw_ref, o_ref, s_ref):
    lb = _lower_bound(lbl_ref[...])
    zf = zf_ref[...]
    f = lb + (1.0 - lb) * _sigmoid(zf)
    k = (1.0 - lb) * _sigmoid(-zf)
    q = q_ref[...].astype(F32)
    v = v_ref[...].astype(F32)
    g = g_ref[...].astype(F32)
    nw = nw_ref[...]
    d = REC_HEAD_DIM
    ri = lax.broadcasted_iota(jnp.int32, (d, d), 0)
    for h in range(REC_HEADS):
        ln = slice(h * d, (h + 1) * d)
        m = jnp.where(ri == 0, f[:, ln], jnp.where(ri == 1, k[:, ln], jnp.where(ri == 2, q[:, ln], 0.0)))
        mt = m.T
        s_new = mt[:, 0:1] * s0_ref[h] + mt[:, 1:2] * v[:, ln]
        s_ref[h] = s_new
        o = jnp.sum(mt[:, 2:3] * s_new, axis=0, keepdims=True)
        o_ref[:, ln] = _rec_out(o, nw[:, ln], g[:, ln]).astype(o_ref.dtype)


def _rec_step(rq, zf, rv, rg, s0, lb_logits, nw_t):
    b = rq.shape[0]
    w = rq.shape[-1]
    tok = pl.BlockSpec((None, 1, w), lambda i: (i, 0, 0))
    st = pl.BlockSpec((None, REC_HEADS, REC_HEAD_DIM, REC_HEAD_DIM), lambda i: (i, 0, 0, 0))
    full = lambda a: pl.BlockSpec(a.shape, lambda i: (0,) * a.ndim)
    return pl.pallas_call(
        _rec_step_kernel,
        grid=(b,),
        in_specs=[tok, tok, tok, tok, st, full(lb_logits), full(nw_t)],
        out_specs=(tok, st),
        out_shape=(jax.ShapeDtypeStruct((b, 1, w), BF16),
                   jax.ShapeDtypeStruct(s0.shape, F32)),
        compiler_params=pltpu.CompilerParams(dimension_semantics=("arbitrary",),
                                             vmem_limit_bytes=VMEM_LIMIT),
        name="rec_step",
    )(rq, zf, rv, rg, s0, lb_logits, nw_t)


def _subln(o, w, lam_init):
    ms = jnp.mean(o * o, axis=-1, keepdims=True)
    return o * lax.rsqrt(ms + EPS) * w * (1.0 - lam_init)


def _lane_block_max(x):
    blocks = [x[:, c * LANES:(c + 1) * LANES] for c in range(x.shape[1] // LANES)]
    return functools.reduce(jnp.maximum, blocks)


def _attn_prompt_kernel(lam_init, lp_ref, q_ref, kt_ref, v_ref, sw_ref, o_ref,
                        qs_ref, kb_ref, vb_ref, acc_ref):
    i = pl.program_id(2)
    tq = q_ref.shape[0]
    seq = kt_ref.shape[1]
    hd = ATT_V_DIM

    @pl.when(i == 0)
    def _():
        kb_ref[...] = kt_ref[...].astype(BF16)
        head = pl.program_id(1)
        vb_ref[:, 0:hd] = v_ref[pl.ds(head, seq, stride=ATT_HEADS), :].astype(BF16)
        vb_ref[:, hd:2 * hd] = jnp.ones((seq, hd), BF16)

    q = q_ref[...]
    lane = lax.broadcasted_iota(jnp.int32, q.shape, 1)
    first = lane < ATT_HEAD_DIM
    zero = jnp.zeros_like(q)
    qs_ref[0:tq, :] = jnp.where(first, q, zero)
    qs_ref[tq:2 * tq, :] = jnp.where(first, zero, q)
    neg = jnp.finfo(F32).min
    rows = min(ATT_ROWS, tq)
    n_chunks = 2 * tq // rows

    ri = lax.broadcasted_iota(jnp.int32, (rows, rows), 0)
    ci = lax.broadcasted_iota(jnp.int32, (rows, rows), 1)
    causal = ri >= ci

    def run(n_off):
        for r in range(n_chunks):
            row0 = r * rows
            n_full = n_off + row0 % tq
            qc = qs_ref[row0:row0 + rows, :]
            s_d = jnp.where(causal, _dot(qc, kb_ref[:, n_full:n_full + rows]), neg)
            mx = _lane_block_max(s_d)
            if n_full:
                s_o = _dot(qc, kb_ref[:, 0:n_full])
                mx = jnp.maximum(mx, _lane_block_max(s_o))
            m = jnp.max(mx, axis=-1, keepdims=True)
            acc = _dot(jnp.exp2(s_d - m).astype(BF16), vb_ref[n_full:n_full + rows, :])
            if n_full:
                acc = acc + _dot(jnp.exp2(s_o - m).astype(BF16), vb_ref[0:n_full, :])
            acc_ref[row0:row0 + rows, :] = acc

    for c in range(seq // tq):
        pl.when(i == c)(functools.partial(run, c * tq))

    lam = _lam_from_params(lp_ref[...], lam_init)
    a0 = acc_ref[0:tq, :]
    a1 = acc_ref[tq:2 * tq, :]
    o = a0[:, 0:hd] / a0[:, hd:2 * hd] - lam * (a1[:, 0:hd] / a1[:, hd:2 * hd])
    o_ref[...] = _subln(o, sw_ref[...], lam_init).astype(o_ref.dtype)


def _attn_prompt(lam_params, q, kt, v, subln_w, lam_init):
    b, l, w = q.shape
    tq = min(ATT_TILE, l)
    assert l % tq == 0
    hd = ATT_V_DIM
    qspec = pl.BlockSpec((None, tq, hd), lambda bi, h, i: (bi, i, h))
    ktspec = pl.BlockSpec((None, hd, l), lambda bi, h, i: (bi, h, 0))
    vspec = pl.BlockSpec((None, l * ATT_HEADS, hd), lambda bi, h, i: (bi, 0, 0))
    full = lambda a: pl.BlockSpec(a.shape, lambda bi, h, i: (0,) * a.ndim)
    return pl.pallas_call(
        functools.partial(_attn_prompt_kernel, lam_init),
        grid=(b, ATT_HEADS, l // tq),
        in_specs=[full(lam_params), qspec, ktspec, vspec, full(subln_w)],
        out_specs=qspec,
        out_shape=jax.ShapeDtypeStruct((b, l, w), BF16),
        scratch_shapes=[pltpu.VMEM((2 * tq, hd), BF16), pltpu.VMEM((hd, l), BF16),
                        pltpu.VMEM((l, 2 * hd), BF16), pltpu.VMEM((2 * tq, 2 * hd), F32)],
        compiler_params=pltpu.CompilerParams(
            dimension_semantics=("arbitrary", "arbitrary", "arbitrary"),
            vmem_limit_bytes=VMEM_LIMIT),
        name="attn_prompt",
    )(lam_params, q, kt, v, subln_w)


def _seg_scores(feat_by_tok, q_bcast):
    n = feat_by_tok.shape[1]
    prod = feat_by_tok * q_bcast
    return jnp.sum(prod.reshape(2 * ATT_HEADS, ATT_HEAD_DIM, n), axis=1)


def _attn_sample_kernel(lam_init, pt_ref, lp_ref, qb_ref, knb_ref, vn_ref, sw_ref, ck_ref, cv_ref,
                        o_ref, kbuf, vbuf, sem_k, sem_v):
    b = pl.program_id(0)
    nb = pl.num_programs(0)
    n_pages = pt_ref.shape[1]
    slots = kbuf.shape[0]
    ahead = slots - 1
    grp = kbuf.shape[1]
    n_groups = n_pages // grp
    page = kbuf.shape[3]
    total = nb * n_groups

    def copies(n):
        bi = n // n_groups
        gi = n % n_groups
        slot = n % slots
        out = []
        for j in range(grp):
            pg = pt_ref[bi, gi * grp + j]
            out.append(pltpu.make_async_copy(ck_ref.at[pg], kbuf.at[slot, j], sem_k.at[slot]))
            out.append(pltpu.make_async_copy(cv_ref.at[pg], vbuf.at[slot, j], sem_v.at[slot]))
        return out

    @pl.when(b == 0)
    def _():
        for n in range(ahead):
            for cp in copies(n):
                cp.start()

    qb = qb_ref[...]
    m0 = _seg_scores(knb_ref[...], qb)[:, 0:1]
    l0 = jnp.ones_like(m0)
    acc0 = tuple(jnp.broadcast_to(vn_ref[:, h * ATT_V_DIM:(h + 1) * ATT_V_DIM], (2 * ATT_HEADS, ATT_V_DIM))
                 for h in range(ATT_HEADS))

    def body(gi, carry):
        m, l, acc = carry
        n = b * n_groups + gi
        slot = n % slots

        @pl.when(n + ahead < total)
        def _():
            for cp in copies(n + ahead):
                cp.start()

        for cp in copies(n):
            cp.wait()
        s = jnp.concatenate([_seg_scores(kbuf[slot, j], qb) for j in range(grp)], axis=1)
        m_new = jnp.maximum(m, jnp.max(s, axis=-1, keepdims=True))
        p = jnp.exp2(s - m_new)
        alpha = jnp.exp2(m - m_new)
        l_new = alpha * l + jnp.sum(p, axis=-1, keepdims=True)
        pb = p.astype(BF16)
        new_acc = []
        for h in range(ATT_HEADS):
            a = alpha * acc[h]
            for j in range(grp):
                vh = vbuf[slot, j, pl.ds(h, page, stride=ATT_HEADS), :].astype(BF16)
                a = a + _dot(pb[:, j * page:(j + 1) * page], vh)
            new_acc.append(a)
        return m_new, l_new, tuple(new_acc)

    m, l, acc = lax.fori_loop(0, n_groups, body, (m0, l0, acc0))
    lam = _lam_from_params(lp_ref[...], lam_init)
    sw = sw_ref[...]
    for h in range(ATT_HEADS):
        r0, r1 = 2 * h, 2 * h + 1
        o = acc[h][r0:r0 + 1, :] / l[r0:r0 + 1, :] - lam * (acc[h][r1:r1 + 1, :] / l[r1:r1 + 1, :])
        o_ref[:, h * ATT_V_DIM:(h + 1) * ATT_V_DIM] = _subln(o, sw, lam_init).astype(o_ref.dtype)


def _attn_sample(lam_params, q_bcast, knew_bcast, v_new, subln_w, cache_kt, cache_vf, page_table, lam_init):
    b = q_bcast.shape[0]
    n_pages = page_table.shape[1]
    page = cache_kt.shape[2]
    grp = min(PAGE_GROUP, n_pages)
    slots = PAGE_SLOTS
    assert n_pages % grp == 0 and b * (n_pages // grp) >= slots - 1
    w = ATT_WIDTH
    bc = pl.BlockSpec((None, w, LANES), lambda bi, pt: (bi, 0, 0))
    tok = pl.BlockSpec((None, 1, w), lambda bi, pt: (bi, 0, 0))
    full = lambda a: pl.BlockSpec(a.shape, lambda bi, pt: (0,) * a.ndim)
    hbm = pl.BlockSpec(memory_space=pl.ANY)
    grid_spec = pltpu.PrefetchScalarGridSpec(
        num_scalar_prefetch=1,
        grid=(b,),
        in_specs=[full(lam_params), bc, bc, tok, full(subln_w), hbm, hbm],
        out_specs=tok,
        scratch_shapes=[pltpu.VMEM((slots, grp, w, page), F32),
                        pltpu.VMEM((slots, grp, page * ATT_HEADS, ATT_V_DIM), F32),
                        pltpu.SemaphoreType.DMA((slots,)), pltpu.SemaphoreType.DMA((slots,))],
    )
    return pl.pallas_call(
        functools.partial(_attn_sample_kernel, lam_init),
        grid_spec=grid_spec,
        out_shape=jax.ShapeDtypeStruct((b, 1, w), BF16),
        compiler_params=pltpu.CompilerParams(dimension_semantics=("arbitrary",),
                                             vmem_limit_bytes=VMEM_LIMIT),
        name="attn_sample",
    )(page_table, lam_params, q_bcast, knew_bcast, v_new, subln_w, cache_kt, cache_vf)


def _mix_kernel(x_ref, orec_ref, oatt_ref, gate_ref, wur_ref, wua_ref, wo_ref, ln2_ref, wr_ref, br_ref,
                hp_ref, hn_ref, route_ref, hist_ref):
    y_rec = _dot(orec_ref[...], wur_ref[...])
    y_att = _dot(oatt_ref[...], wua_ref[...])
    g_rec = gate_ref[:, :D_MODEL].astype(F32)
    g_att = gate_ref[:, D_MODEL:].astype(F32)
    mix = (g_rec * y_rec + g_att * y_att).astype(BF16)
    hp = x_ref[...] + _dot(mix, wo_ref[...])
    hp_ref[...] = hp
    ms = jnp.mean(hp * hp, axis=-1, keepdims=True)
    hn = (hp * lax.rsqrt(ms + EPS) * ln2_ref[...]).astype(BF16)
    hn_ref[...] = hn
    logits = _dot(hn, wr_ref[...]) + br_ref[...]
    lt = logits.T[0:N_EXPERTS, :]
    row = lax.broadcasted_iota(jnp.int32, lt.shape, 0).astype(F32)
    vals, idxs = [], []
    for _ in range(TOP_K):
        mk = jnp.max(lt, axis=0, keepdims=True)
        ik = jnp.min(jnp.where(lt == mk, row, float(N_EXPERTS)), axis=0, keepdims=True)
        vals.append(mk)
        idxs.append(ik)
        lt = jnp.where(row == ik, -jnp.inf, lt)
    es = [jnp.exp(v - vals[0]) for v in vals]
    denom = es[0] + es[1] + es[2] + es[3]
    route_ref[...] = jnp.concatenate(idxs + [e / denom for e in es], axis=0)
    hits = jnp.where(row == idxs[0], 1.0, 0.0)
    for kk in range(1, TOP_K):
        hits = hits + jnp.where(row == idxs[kk], 1.0, 0.0)
    hist_ref[...] = jnp.sum(hits, axis=1, keepdims=True)


def _mix(x2d, orec, oatt, gates, wur, wua, wo, ln2, wr, br):
    t = x2d.shape[0]
    tm = min(TOKEN_TILE, t)
    assert t % tm == 0
    row = lambda w: pl.BlockSpec((tm, w), lambda i: (i, 0))
    full = lambda a: pl.BlockSpec(a.shape, lambda i: (0,) * a.ndim)
    return pl.pallas_call(
        _mix_kernel,
        grid=(t // tm,),
        in_specs=[row(D_MODEL), row(REC_WIDTH), row(ATT_WIDTH), row(2 * D_MODEL),
                  full(wur), full(wua), full(wo), full(ln2), full(wr), full(br)],
        out_specs=(row(D_MODEL), row(D_MODEL),
                   pl.BlockSpec((2 * TOP_K, tm), lambda i: (0, i)),
                   pl.BlockSpec((None, N_EXPERTS, 1), lambda i: (i, 0, 0))),
        out_shape=(jax.ShapeDtypeStruct((t, D_MODEL), F32),
                   jax.ShapeDtypeStruct((t, D_MODEL), BF16),
                   jax.ShapeDtypeStruct((2 * TOP_K, t), F32),
                   jax.ShapeDtypeStruct((t // tm, N_EXPERTS, 1), F32)),
        compiler_params=pltpu.CompilerParams(dimension_semantics=("arbitrary",),
                                             vmem_limit_bytes=VMEM_LIMIT),
        name="mix_route",
    )(x2d, orec, oatt, gates, wur, wua, wo, ln2, wr, br)


def _pair_split_matrix():
    m = np.zeros((MXU_DIM, MXU_DIM), np.float32)
    half = MXU_DIM // 2
    m[2 * np.arange(half), np.arange(half)] = 1.0
    m[2 * np.arange(half) + 1, half + np.arange(half)] = 1.0
    return m


def _wprep_kernel(w_ref, p_ref, o_ref):
    for g in range(w_ref.shape[1] // MXU_DIM):
        cols = slice(g * MXU_DIM, (g + 1) * MXU_DIM)
        o_ref[:, cols] = _dot(w_ref[:, cols].astype(BF16), p_ref[...]).astype(o_ref.dtype)


def _prep_up_weights(w_up, split_mat):
    e, d, f2 = w_up.shape
    return pl.pallas_call(
        _wprep_kernel,
        grid=(e,),
        in_specs=[pl.BlockSpec((None, d, f2), lambda i: (i, 0, 0)),
                  pl.BlockSpec(split_mat.shape, lambda i: (0, 0))],
        out_specs=pl.BlockSpec((None, d, f2), lambda i: (i, 0, 0)),
        out_shape=jax.ShapeDtypeStruct((e, d, f2), BF16),
        compiler_params=pltpu.CompilerParams(dimension_semantics=("arbitrary",),
                                             vmem_limit_bytes=VMEM_LIMIT),
        name="moe_wprep",
    )(w_up, split_mat)


def _moe_kernel(wt_ref, we_ref, lo_ref, hi_ref, first_ref, x_ref, wu_ref, bu_ref, wd_ref, bd_ref, rw_ref,
                y_ref):
    del we_ref
    w = pl.program_id(0)
    lo = lo_ref[w]
    hi = hi_ref[w]

    tm = y_ref.shape[0]

    @pl.when(first_ref[w] == 1)
    def _():
        y_ref[...] = jnp.zeros_like(y_ref)

    @pl.when(hi > lo)
    def _():
        rw_col = jnp.broadcast_to(rw_ref[...], (LANES, tm)).T[:, 0:1]
        row = lax.broadcasted_iota(jnp.int32, (tm, 1), 0) + wt_ref[w] * tm
        scale = jnp.where((row >= lo) & (row < hi), rw_col, 0.0)
        half = MXU_DIM // 2
        n_grp = 2 * D_FF // MXU_DIM
        h = _dot(x_ref[...], wu_ref[...]) + bu_ref[...]
        glu = jnp.concatenate([h[:, g * MXU_DIM:g * MXU_DIM + half] for g in range(n_grp)], axis=1)
        lin = jnp.concatenate([h[:, g * MXU_DIM + half:(g + 1) * MXU_DIM] for g in range(n_grp)], axis=1)
        glu = jnp.minimum(glu, SWIGLU_LIMIT)
        lin = jnp.clip(lin, -SWIGLU_LIMIT, SWIGLU_LIMIT)
        a = glu * _sigmoid(SWIGLU_ALPHA * glu) * (lin + 1.0)
        y = scale * (_dot(a.astype(BF16), wd_ref[...]) + bd_ref[...])
        y_ref[...] = y_ref[...] + y.astype(y_ref.dtype)


def _moe_ffn(work, x_sorted, wu, bu, wd, bd, row_w, tm):
    n_rows = x_sorted.shape[0]
    assert n_rows % tm == 0
    row = lambda c: pl.BlockSpec((tm, c), lambda w, wt, we, lo, hi, fi: (wt[w], 0))
    per_e = lambda r, c: pl.BlockSpec((None, r, c), lambda w, wt, we, lo, hi, fi: (we[w], 0, 0))
    grid_spec = pltpu.PrefetchScalarGridSpec(
        num_scalar_prefetch=5,
        grid=(work[0].shape[0],),
        in_specs=[row(D_MODEL), per_e(D_MODEL, 2 * D_FF), per_e(1, 2 * D_FF),
                  per_e(D_FF, D_MODEL), per_e(1, D_MODEL),
                  pl.BlockSpec((None, 1, tm), lambda w, wt, we, lo, hi, fi: (wt[w], 0, 0))],
        out_specs=row(D_MODEL),
    )
    return pl.pallas_call(
        _moe_kernel,
        grid_spec=grid_spec,
        out_shape=jax.ShapeDtypeStruct((n_rows, D_MODEL), BF16),
        compiler_params=pltpu.CompilerParams(dimension_semantics=("arbitrary",),
                                             vmem_limit_bytes=VMEM_LIMIT),
        name="moe_ffn",
    )(*work, x_sorted, wu, bu, wd, bd, row_w)


def _combine_kernel(hp_ref, y_ref, o_ref):
    moe = y_ref[0].astype(F32)
    for kk in range(1, TOP_K):
        moe = moe + y_ref[kk].astype(F32)
    o_ref[...] = hp_ref[...] + moe


def _combine(hp, y_rows):
    t = hp.shape[0]
    tm = min(TOKEN_TILE, t)
    assert t % tm == 0
    return pl.pallas_call(
        _combine_kernel,
        grid=(t // tm,),
        in_specs=[pl.BlockSpec((tm, D_MODEL), lambda i: (i, 0)),
                  pl.BlockSpec((TOP_K, tm, D_MODEL), lambda i: (0, i, 0))],
        out_specs=pl.BlockSpec((tm, D_MODEL), lambda i: (i, 0)),
        out_shape=jax.ShapeDtypeStruct((t, D_MODEL), F32),
        compiler_params=pltpu.CompilerParams(dimension_semantics=("arbitrary",),
                                             vmem_limit_bytes=VMEM_LIMIT),
        name="moe_combine",
    )(hp, y_rows)


def _moe(hn, route, hist, wu, bu, wd, bd, run_after=None):
    t = hn.shape[0]
    n = t * TOP_K
    tm = min(MOE_TILE, t)
    assert n % tm == 0
    n_tiles = n // tm
    shift = max(1, (n - 1).bit_length())
    assert N_EXPERTS << shift < 2 ** 31
    e_flat = route[:TOP_K].astype(jnp.int32).reshape(n)
    w_flat = route[TOP_K:2 * TOP_K].reshape(n)
    ar = jnp.arange(n, dtype=jnp.int32)
    keys = lax.sort((e_flat << shift) | ar, is_stable=False)
    order = keys & ((1 << shift) - 1)
    _, pos = lax.sort((order, ar), num_keys=1, is_stable=False)
    counts = jnp.sum(hist, axis=(0, 2)).astype(jnp.int32)
    c_end = jnp.cumsum(counts)
    c_start = c_end - counts
    cuts = lax.sort(jnp.concatenate([jnp.arange(n_tiles, dtype=jnp.int32) * tm, c_start[1:]]), is_stable=False)
    lo = cuts
    hi = jnp.concatenate([cuts[1:], jnp.full((1,), n, jnp.int32)])
    w_tile = jnp.minimum(lo // tm, n_tiles - 1)
    w_expert = jnp.minimum(jnp.sum((c_end[None, :] <= lo[:, None]).astype(jnp.int32), axis=1), N_EXPERTS - 1)
    first = ((lo == w_tile * tm) & (hi > lo)).astype(jnp.int32)
    token = (order & (t - 1)) if t & (t - 1) == 0 else order % t
    x_sorted = hn.at[token].get(mode="promise_in_bounds")
    row_w = w_flat.at[order].get(mode="promise_in_bounds").reshape(n_tiles, 1, tm)
    if run_after is not None:
        row_w, _ = lax.optimization_barrier((row_w, run_after))
    y_sorted = _moe_ffn((w_tile, w_expert, lo, hi, first), x_sorted, wu, bu, wd, bd, row_w, tm)
    return y_sorted.at[pos].get(mode="promise_in_bounds").reshape(TOP_K, t, D_MODEL)


def _seg_mean_matrix():
    g = np.arange(ATT_WIDTH) // ATT_HEAD_DIM
    return jnp.asarray((g[:, None] == g[None, :]).astype(np.float32) / ATT_HEAD_DIM, dtype=BF16)


def kernel(x_prompt, x_sample, cache_k, cache_v, state_rec, page_table, ln1_w, w_in, rec_lb_logits,
           rec_norm_w, w_up_rec, q_norm_w, k_norm_w, lambda_q1, lambda_k1, lambda_q2, lambda_k2,
           att_subln_w, w_up_att, w_out, ln2_w, w_router, b_router, w_exp_up, b_exp_up,
           w_exp_down, b_exp_down):
    depth = ln1_w.shape[0]
    assert depth == 1 and x_sample.shape[1] == 1
    layer = 0
    bp, lp_, d = x_prompt.shape
    bs = x_sample.shape[0]
    tp = bp * lp_
    lam_init = _lambda_init(layer)

    w_in_b = w_in[layer].astype(BF16)
    ln1 = ln1_w[layer].reshape(1, d)
    ln2 = ln2_w[layer].reshape(1, d)
    qn_t = jnp.tile(q_norm_w[layer], ATT_WIDTH // ATT_HEAD_DIM).reshape(1, ATT_WIDTH)
    kn_t = jnp.tile(k_norm_w[layer], ATT_WIDTH // ATT_HEAD_DIM).reshape(1, ATT_WIDTH)
    nw_t = jnp.tile(rec_norm_w[layer], REC_HEADS).reshape(1, REC_WIDTH)
    subln = att_subln_w[layer].reshape(1, ATT_V_DIM)
    lam_params = jnp.stack([lambda_q1[layer], lambda_k1[layer], lambda_q2[layer], lambda_k2[layer]]).astype(F32)
    lb_logits = rec_lb_logits.astype(F32)[layer:layer + 2]
    wur = w_up_rec[layer].astype(BF16)
    wua = w_up_att[layer].astype(BF16)
    wo = w_out[layer].astype(BF16)
    wr = jnp.zeros((d, LANES), BF16).at[:, :N_EXPERTS].set(w_router[layer].astype(BF16))
    br = jnp.full((1, LANES), -1e30, F32).at[0, :N_EXPERTS].set(b_router[layer].astype(F32))
    wu = _prep_up_weights(w_exp_up[layer], jnp.asarray(_pair_split_matrix(), dtype=BF16))
    half = MXU_DIM // 2
    bu = (b_exp_up[layer].reshape(N_EXPERTS, 2 * D_FF // MXU_DIM, half, 2)
          .transpose(0, 1, 3, 2).reshape(N_EXPERTS, 1, 2 * D_FF))
    wd = w_exp_down[layer].astype(BF16)
    bd = b_exp_down[layer].reshape(N_EXPERTS, 1, D_MODEL)
    seg_mean = _seg_mean_matrix()

    xp2 = x_prompt.reshape(tp, d)
    rq, zf, rv, rg, aq, akt, av, gates = _in_proj(xp2, ln1, w_in_b, seg_mean, qn_t, kn_t, seq_len=lp_)
    seq = lambda a: a.reshape(bp, lp_, a.shape[-1])
    o_rec, s_p = _gla_prompt(seq(rq), seq(zf), seq(rv), seq(rg), lb_logits, nw_t)
    o_att = _attn_prompt(lam_params, seq(aq), akt, av.reshape(bp, lp_ * ATT_HEADS, ATT_V_DIM), subln, lam_init)
    hp, hn_p, route_p, hist_p = _mix(xp2, o_rec.reshape(tp, REC_WIDTH), o_att.reshape(tp, ATT_WIDTH), gates,
                                     wur, wua, wo, ln2, wr, br)

    xs2 = x_sample.reshape(bs, d)
    rq_s, zf_s, rv_s, rg_s, aq_s, ak_s, av_s, gates_s = _in_proj(xs2, ln1, w_in_b, seg_mean, qn_t, kn_t)
    tok = lambda a: a.reshape(bs, 1, a.shape[-1])
    o_rec_s, s_s = _rec_step(tok(rq_s), tok(zf_s), tok(rv_s), tok(rg_s), state_rec[layer].astype(F32),
                             lb_logits, nw_t)
    n_pool, page = cache_k.shape[1], cache_k.shape[2]
    ckt = jnp.transpose(cache_k[layer].reshape(n_pool, page, ATT_WIDTH), (0, 2, 1))
    cvf = cache_v[layer].reshape(n_pool, page * ATT_HEADS, ATT_V_DIM)
    q_bcast = jnp.broadcast_to(aq_s.astype(F32)[:, :, None], (bs, ATT_WIDTH, LANES))
    kn_bcast = jnp.broadcast_to(ak_s[:, :, None], (bs, ATT_WIDTH, LANES))
    o_att_s = _attn_sample(lam_params, q_bcast, kn_bcast, av_s.reshape(bs, 1, ATT_WIDTH), subln, ckt, cvf,
                           page_table.astype(jnp.int32), lam_init)
    hs, hn_s, route_s, hist_s = _mix(xs2, o_rec_s.reshape(bs, REC_WIDTH), o_att_s.reshape(bs, ATT_WIDTH),
                                     gates_s, wur, wua, wo, ln2, wr, br)

    y_prompt = _combine(hp, _moe(hn_p, route_p, hist_p, wu, bu, wd, bd,
                                 run_after=(s_s, q_bcast, kn_bcast))).reshape(bp, lp_, d)
    y_sample = _combine(hs, _moe(hn_s, route_s, hist_s, wu, bu, wd, bd)).reshape(bs, 1, d)

    new_k_prompt = jnp.transpose(akt.reshape(bp, ATT_HEADS, 2, ATT_HEAD_DIM, lp_), (0, 4, 1, 2, 3))[None]
    new_v_prompt = av.reshape(1, bp, lp_, ATT_HEADS, ATT_V_DIM)
    new_rec_prompt = s_p.reshape(1, bp, REC_HEADS, REC_HEAD_DIM, REC_HEAD_DIM).astype(state_rec.dtype)
    new_k_sample = ak_s.reshape(1, bs, 1, ATT_HEADS, 2, ATT_HEAD_DIM)
    new_v_sample = av_s.reshape(1, bs, 1, ATT_HEADS, ATT_V_DIM)
    new_rec_sample = s_s.reshape(1, bs, REC_HEADS, REC_HEAD_DIM, REC_HEAD_DIM).astype(state_rec.dtype)
    return (y_prompt, y_sample, new_k_prompt, new_v_prompt, new_rec_prompt,
            new_k_sample, new_v_sample, new_rec_sample)
```

```python
import functools
import math

import numpy as np
import jax
import jax.numpy as jnp
from jax import lax
from jax.experimental import pallas as pl
from jax.experimental.pallas import tpu as pltpu

F32 = jnp.float32
BF16 = jnp.bfloat16

D_MODEL = 1024
REC_WIDTH = 512
REC_HEAD_DIM = 128
REC_HEADS = 4
ATT_WIDTH = 512
ATT_HEAD_DIM = 64
ATT_V_DIM = 128
ATT_HEADS = 4
N_EXPERTS = 32
TOP_K = 4
D_FF = 1024
SWIGLU_LIMIT = 7.0
SWIGLU_ALPHA = 1.702
CHUNK = 64
EPS = 1e-6
N_IN = 4 * REC_WIDTH + 3 * ATT_WIDTH + 2 * D_MODEL
LANES = 128
SUBLANES = 8
MXU_DIM = 256
VMEM_LIMIT = 48 * 1024 * 1024
LOG2E = math.log2(math.e)
Q_SCALE = (ATT_HEAD_DIM ** -0.5) * LOG2E

TOKEN_TILE = 512
GLA_BLOCK = 512
PAGE_SLOTS = 3
ATT_TILE = 512
ATT_ROWS = 256
MOE_TILE = 512
PAGE_GROUP = 8


def _sigmoid(x):
    return 1.0 / (1.0 + jnp.exp(-x))


def _dot(a, b):
    return jnp.dot(a, b, preferred_element_type=F32)


def _dot_nt(a, b):
    return lax.dot_general(a, b, (((1,), (1,)), ((), ())), preferred_element_type=F32)


def _dot_tn(a, b):
    return lax.dot_general(a, b, (((0,), (0,)), ((), ())), preferred_element_type=F32)


def _split_dot(x, m_bf16, terms):
    acc = None
    r = x
    for t in range(terms):
        p = r.astype(BF16)
        d = _dot(p, m_bf16)
        acc = d if acc is None else acc + d
        if t + 1 < terms:
            r = r - p.astype(F32)
    return acc


def _lambda_init(layer):
    return 0.8 - 0.6 * math.exp(-0.3 * layer)


def _lam_from_params(lp, lam_init):
    a = jnp.sum(lp[0:1, :] * lp[1:2, :], axis=-1, keepdims=True)
    b = jnp.sum(lp[2:3, :] * lp[3:4, :], axis=-1, keepdims=True)
    return jnp.exp(a) - jnp.exp(b) + lam_init


def _lower_bound(lbl):
    a0 = lbl[0:1, :]
    a1 = lbl[1:2, :]
    mx = jnp.maximum(a0, a1)
    e0 = jnp.exp(a0 - mx)
    e1 = jnp.exp(a1 - mx)
    return e0 / (e0 + e1)


def _in_proj_kernel(k_feature_major, x_ref, ln_ref, w_ref, seg_ref, qn_ref, kn_ref,
                    rq_ref, zf_ref, rv_ref, rg_ref, aq_ref, ak_ref, av_ref, gate_ref):
    x = x_ref[...]
    ms = jnp.mean(x * x, axis=-1, keepdims=True)
    xn = (x * lax.rsqrt(ms + EPS) * ln_ref[...]).astype(BF16)

    def proj(lo, hi):
        return _dot(xn, w_ref[:, lo:hi])

    def seg_norm(a, w):
        ms_seg = _split_dot(a * a, seg_ref[...], 1)
        return a * lax.rsqrt(ms_seg + EPS) * w

    r = REC_WIDTH
    a0 = 4 * r
    g0 = a0 + 3 * ATT_WIDTH
    gate_ref[...] = _sigmoid(proj(g0, g0 + 2 * D_MODEL)).astype(BF16)
    aq = seg_norm(proj(a0, a0 + ATT_WIDTH), qn_ref[...])
    aq_ref[...] = (aq * Q_SCALE).astype(BF16)
    ak = seg_norm(proj(a0 + ATT_WIDTH, a0 + 2 * ATT_WIDTH), kn_ref[...])
    ak_ref[...] = ak.T if k_feature_major else ak
    av = proj(a0 + 2 * ATT_WIDTH, a0 + 3 * ATT_WIDTH)
    rows = av.shape[0]
    for h in range(ATT_HEADS):
        av_ref[pl.ds(h, rows, stride=ATT_HEADS), :] = av[:, h * ATT_V_DIM:(h + 1) * ATT_V_DIM]
    zf_ref[...] = proj(r, 2 * r)
    rq_ref[...] = proj(0, r).astype(BF16)
    rv_ref[...] = proj(2 * r, 3 * r).astype(BF16)
    rg_ref[...] = proj(3 * r, 4 * r).astype(BF16)


def _in_proj(x2d, ln_w, w_in_bf16, seg_mat, qn_t, kn_t, seq_len=None):
    t = x2d.shape[0]
    tm = min(TOKEN_TILE, t)
    assert t % tm == 0
    row = lambda w: pl.BlockSpec((tm, w), lambda i: (i, 0))
    full = lambda a: pl.BlockSpec(a.shape, lambda i: (0,) * a.ndim)
    if seq_len is None:
        k_shape = jax.ShapeDtypeStruct((t, ATT_WIDTH), F32)
        k_spec = row(ATT_WIDTH)
    else:
        assert seq_len % tm == 0 and t % seq_len == 0
        per_seq = seq_len // tm
        k_shape = jax.ShapeDtypeStruct((t // seq_len, ATT_WIDTH, seq_len), F32)
        k_spec = pl.BlockSpec((None, ATT_WIDTH, tm), lambda i: (i // per_seq, 0, i % per_seq))
    out_shapes = (
        jax.ShapeDtypeStruct((t, REC_WIDTH), BF16),
        jax.ShapeDtypeStruct((t, REC_WIDTH), F32),
        jax.ShapeDtypeStruct((t, REC_WIDTH), BF16),
        jax.ShapeDtypeStruct((t, REC_WIDTH), BF16),
        jax.ShapeDtypeStruct((t, ATT_WIDTH), BF16),
        k_shape,
        jax.ShapeDtypeStruct((t * ATT_HEADS, ATT_V_DIM), F32),
        jax.ShapeDtypeStruct((t, 2 * D_MODEL), BF16),
    )
    v_spec = pl.BlockSpec((tm * ATT_HEADS, ATT_V_DIM), lambda i: (i, 0))
    out_specs = tuple(k_spec if s is k_shape else row(s.shape[1]) for s in out_shapes)
    out_specs = out_specs[:6] + (v_spec,) + out_specs[7:]
    return pl.pallas_call(
        functools.partial(_in_proj_kernel, seq_len is not None),
        grid=(t // tm,),
        in_specs=[row(D_MODEL), full(ln_w), full(w_in_bf16), full(seg_mat), full(qn_t), full(kn_t)],
        out_specs=out_specs,
        out_shape=out_shapes,
        compiler_params=pltpu.CompilerParams(dimension_semantics=("arbitrary",),
                                             vmem_limit_bytes=VMEM_LIMIT),
        name="in_proj",
    )(x2d, ln_w, w_in_bf16, seg_mat, qn_t, kn_t)


def _rec_out(o, nw, g):
    ms = jnp.mean(o * o, axis=-1, keepdims=True)
    return o * lax.rsqrt(ms + EPS) * nw * (g * _sigmoid(g))


def _cumsum_rows(x):
    n = x.shape[0]
    row = lax.broadcasted_iota(jnp.int32, x.shape, 0)
    shift = 1
    while shift < n:
        x = x + jnp.where(row >= shift, pltpu.roll(x, shift, axis=0), 0.0)
        shift *= 2
    return x


def _gla_kernel(q_ref, zf_ref, v_ref, g_ref, lbl_ref, nw_ref, o_ref, sfin_ref, st_ref):
    j = pl.program_id(1)

    @pl.when(j == 0)
    def _():
        st_ref[...] = jnp.zeros_like(st_ref)

    lb = _lower_bound(lbl_ref[...])
    c = CHUNK
    ri = lax.broadcasted_iota(jnp.int32, (c, c), 0)
    ci = lax.broadcasted_iota(jnp.int32, (c, c), 1)
    causal = ri >= ci
    nw = nw_ref[...]
    n_chunks = q_ref.shape[0] // c
    for n in range(n_chunks):
        rows = slice(n * c, (n + 1) * c)
        zf = zf_ref[rows, :]
        log_f = jnp.log(lb + (1.0 - lb) * _sigmoid(zf))
        k = (1.0 - lb) * _sigmoid(-zf)
        b = _cumsum_rows(log_f)
        b_last = b[c - 1:c, :]
        e_pos = jnp.exp(b)
        e_neg = jnp.exp(-b)
        e_tail = jnp.exp(b_last - b)
        decay = jnp.exp(b_last)
        q_e = (q_ref[rows, :].astype(F32) * e_pos).astype(BF16)
        k_e = (k * e_neg).astype(BF16)
        k_t = (k * e_tail).astype(BF16)
        v = v_ref[rows, :]
        g = g_ref[rows, :].astype(F32)
        for h in range(REC_HEADS):
            ln = slice(h * REC_HEAD_DIM, (h + 1) * REC_HEAD_DIM)
            s_t = st_ref[h]
            sc = jnp.where(causal, _dot_nt(q_e[:, ln], k_e[:, ln]), 0.0)
            o = _dot(sc.astype(BF16), v[:, ln]) + _dot_nt(q_e[:, ln], s_t.astype(BF16))
            st_ref[h] = decay[:, ln] * s_t + _dot_tn(v[:, ln], k_t[:, ln])
            o_ref[rows, ln] = _rec_out(o, nw[:, ln], g[:, ln]).astype(o_ref.dtype)

    @pl.when(j == pl.num_programs(1) - 1)
    def _():
        for h in range(REC_HEADS):
            sfin_ref[h] = st_ref[h].T


def _gla_prompt(rq, zf, rv, rg, lb_logits, nw_t):
    b, l, w = rq.shape
    assert l % CHUNK == 0
    lb_rows = min(GLA_BLOCK, l)
    assert l % lb_rows == 0
    seq = pl.BlockSpec((None, lb_rows, w), lambda i, j: (i, j, 0))
    full = lambda a: pl.BlockSpec(a.shape, lambda i, j: (0,) * a.ndim)
    return pl.pallas_call(
        _gla_kernel,
        grid=(b, l // lb_rows),
        in_specs=[seq, seq, seq, seq, full(lb_logits), full(nw_t)],
        out_specs=(seq, pl.BlockSpec((None, REC_HEADS, REC_HEAD_DIM, REC_HEAD_DIM),
                                     lambda i, j: (i, 0, 0, 0))),
        out_shape=(jax.ShapeDtypeStruct((b, l, w), BF16),
                   jax.ShapeDtypeStruct((b, REC_HEADS, REC_HEAD_DIM, REC_HEAD_DIM), F32)),
        scratch_shapes=[pltpu.VMEM((REC_HEADS, REC_HEAD_DIM, REC_HEAD_DIM), F32)],
        compiler_params=pltpu.CompilerParams(dimension_semantics=("arbitrary", "arbitrary"),
                                             vmem_limit_bytes=VMEM_LIMIT),
        name="gla_prompt",
    )(rq, zf, rv, rg, lb_logits, nw_t)


def _rec_step_kernel(q_ref, zf_ref, v_ref, g_ref, s0_ref, lbl_ref, nw_ref, o_ref, s_ref):
    lb = _lower_bound(lbl_ref[...])
    zf = zf_ref[...]
    f = lb + (1.0 - lb) * _sigmoid(zf)
    k = (1.0 - lb) * _sigmoid(-zf)
    q = q_ref[...].astype(F32)
    v = v_ref[...].astype(F32)
    g = g_ref[...].astype(F32)
    nw = nw_ref[...]
    d = REC_HEAD_DIM
    ri = lax.broadcasted_iota(jnp.int32, (d, d), 0)
    for h in range(REC_HEADS):
        ln = slice(h * d, (h + 1) * d)
        m = jnp.where(ri == 0, f[:, ln], jnp.where(ri == 1, k[:, ln], jnp.where(ri == 2, q[:, ln], 0.0)))
        mt = m.T
        s_new = mt[:, 0:1] * s0_ref[h] + mt[:, 1:2] * v[:, ln]
        s_ref[h] = s_new
        o = jnp.sum(mt[:, 2:3] * s_new, axis=0, keepdims=True)
        o_ref[:, ln] = _rec_out(o, nw[:, ln], g[:, ln]).astype(o_ref.dtype)


def _rec_step(rq, zf, rv, rg, s0, lb_logits, nw_t):
    b = rq.shape[0]
    w = rq.shape[-1]
    tok = pl.BlockSpec((None, 1, w), lambda i: (i, 0, 0))
    st = pl.BlockSpec((None, REC_HEADS, REC_HEAD_DIM, REC_HEAD_DIM), lambda i: (i, 0, 0, 0))
    full = lambda a: pl.BlockSpec(a.shape, lambda i: (0,) * a.ndim)
    return pl.pallas_call(
        _rec_step_kernel,
        grid=(b,),
        in_specs=[tok, tok, tok, tok, st, full(lb_logits), full(nw_t)],
        out_specs=(tok, st),
        out_shape=(jax.ShapeDtypeStruct((b, 1, w), BF16),
                   jax.ShapeDtypeStruct(s0.shape, F32)),
        compiler_params=pltpu.CompilerParams(dimension_semantics=("arbitrary",),
                                             vmem_limit_bytes=VMEM_LIMIT),
        name="rec_step",
    )(rq, zf, rv, rg, s0, lb_logits, nw_t)


def _subln(o, w, lam_init):
    ms = jnp.mean(o * o, axis=-1, keepdims=True)
    return o * lax.rsqrt(ms + EPS) * w * (1.0 - lam_init)


def _lane_block_max(x):
    blocks = [x[:, c * LANES:(c + 1) * LANES] for c in range(x.shape[1] // LANES)]
    return functools.reduce(jnp.maximum, blocks)


def _attn_prompt_kernel(lam_init, lp_ref, q_ref, kt_ref, v_ref, sw_ref, o_ref,
                        qs_ref, kb_ref, vb_ref, acc_ref):
    i = pl.program_id(2)
    tq = q_ref.shape[0]
    seq = kt_ref.shape[1]
    hd = ATT_V_DIM

    @pl.when(i == 0)
    def _():
        kb_ref[...] = kt_ref[...].astype(BF16)
        head = pl.program_id(1)
        vb_ref[:, 0:hd] = v_ref[pl.ds(head, seq, stride=ATT_HEADS), :].astype(BF16)
        vb_ref[:, hd:2 * hd] = jnp.ones((seq, hd), BF16)

    q = q_ref[...]
    lane = lax.broadcasted_iota(jnp.int32, q.shape, 1)
    first = lane < ATT_HEAD_DIM
    zero = jnp.zeros_like(q)
    qs_ref[0:tq, :] = jnp.where(first, q, zero)
    qs_ref[tq:2 * tq, :] = jnp.where(first, zero, q)
    neg = jnp.finfo(F32).min
    rows = min(ATT_ROWS, tq)
    n_chunks = 2 * tq // rows

    ri = lax.broadcasted_iota(jnp.int32, (rows, rows), 0)
    ci = lax.broadcasted_iota(jnp.int32, (rows, rows), 1)
    causal = ri >= ci

    def run(n_off):
        for r in range(n_chunks):
            row0 = r * rows
            n_full = n_off + row0 % tq
            qc = qs_ref[row0:row0 + rows, :]
            s_d = jnp.where(causal, _dot(qc, kb_ref[:, n_full:n_full + rows]), neg)
            mx = _lane_block_max(s_d)
            if n_full:
                s_o = _dot(qc, kb_ref[:, 0:n_full])
                mx = jnp.maximum(mx, _lane_block_max(s_o))
            m = jnp.max(mx, axis=-1, keepdims=True)
            acc = _dot(jnp.exp2(s_d - m).astype(BF16), vb_ref[n_full:n_full + rows, :])
            if n_full:
                acc = acc + _dot(jnp.exp2(s_o - m).astype(BF16), vb_ref[0:n_full, :])
            acc_ref[row0:row0 + rows, :] = acc

    for c in range(seq // tq):
        pl.when(i == c)(functools.partial(run, c * tq))

    lam = _lam_from_params(lp_ref[...], lam_init)
    a0 = acc_ref[0:tq, :]
    a1 = acc_ref[tq:2 * tq, :]
    o = a0[:, 0:hd] / a0[:, hd:2 * hd] - lam * (a1[:, 0:hd] / a1[:, hd:2 * hd])
    o_ref[...] = _subln(o, sw_ref[...], lam_init).astype(o_ref.dtype)


def _attn_prompt(lam_params, q, kt, v, subln_w, lam_init):
    b, l, w = q.shape
    tq = min(ATT_TILE, l)
    assert l % tq == 0
    hd = ATT_V_DIM
    qspec = pl.BlockSpec((None, tq, hd), lambda bi, h, i: (bi, i, h))
    ktspec = pl.BlockSpec((None, hd, l), lambda bi, h, i: (bi, h, 0))
    vspec = pl.BlockSpec((None, l * ATT_HEADS, hd), lambda bi, h, i: (bi, 0, 0))
    full = lambda a: pl.BlockSpec(a.shape, lambda bi, h, i: (0,) * a.ndim)
    return pl.pallas_call(
        functools.partial(_attn_prompt_kernel, lam_init),
        grid=(b, ATT_HEADS, l // tq),
        in_specs=[full(lam_params), qspec, ktspec, vspec, full(subln_w)],
        out_specs=qspec,
        out_shape=jax.ShapeDtypeStruct((b, l, w), BF16),
        scratch_shapes=[pltpu.VMEM((2 * tq, hd), BF16), pltpu.VMEM((hd, l), BF16),
                        pltpu.VMEM((l, 2 * hd), BF16), pltpu.VMEM((2 * tq, 2 * hd), F32)],
        compiler_params=pltpu.CompilerParams(
            dimension_semantics=("arbitrary", "arbitrary", "arbitrary"),
            vmem_limit_bytes=VMEM_LIMIT),
        name="attn_prompt",
    )(lam_params, q, kt, v, subln_w)


def _seg_scores(feat_by_tok, q_bcast):
    n = feat_by_tok.shape[1]
    prod = feat_by_tok * q_bcast
    return jnp.sum(prod.reshape(2 * ATT_HEADS, ATT_HEAD_DIM, n), axis=1)


def _attn_sample_kernel(lam_init, pt_ref, lp_ref, qb_ref, knb_ref, vn_ref, sw_ref, ck_ref, cv_ref,
                        o_ref, kbuf, vbuf, sem_k, sem_v):
    b = pl.program_id(0)
    nb = pl.num_programs(0)
    n_pages = pt_ref.shape[1]
    slots = kbuf.shape[0]
    ahead = slots - 1
    grp = kbuf.shape[1]
    n_groups = n_pages // grp
    page = kbuf.shape[3]
    total = nb * n_groups

    def copies(n):
        bi = n // n_groups
        gi = n % n_groups
        slot = n % slots
        out = []
        for j in range(grp):
            pg = pt_ref[bi, gi * grp + j]
            out.append(pltpu.make_async_copy(ck_ref.at[pg], kbuf.at[slot, j], sem_k.at[slot]))
            out.append(pltpu.make_async_copy(cv_ref.at[pg], vbuf.at[slot, j], sem_v.at[slot]))
        return out

    @pl.when(b == 0)
    def _():
        for n in range(ahead):
            for cp in copies(n):
                cp.start()

    qb = qb_ref[...]
    m0 = _seg_scores(knb_ref[...], qb)[:, 0:1]
    l0 = jnp.ones_like(m0)
    acc0 = tuple(jnp.broadcast_to(vn_ref[:, h * ATT_V_DIM:(h + 1) * ATT_V_DIM], (2 * ATT_HEADS, ATT_V_DIM))
                 for h in range(ATT_HEADS))

    def body(gi, carry):
        m, l, acc = carry
        n = b * n_groups + gi
        slot = n % slots

        @pl.when(n + ahead < total)
        def _():
            for cp in copies(n + ahead):
                cp.start()

        for cp in copies(n):
            cp.wait()
        s = jnp.concatenate([_seg_scores(kbuf[slot, j], qb) for j in range(grp)], axis=1)
        m_new = jnp.maximum(m, jnp.max(s, axis=-1, keepdims=True))
        p = jnp.exp2(s - m_new)
        alpha = jnp.exp2(m - m_new)
        l_new = alpha * l + jnp.sum(p, axis=-1, keepdims=True)
        pb = p.astype(BF16)
        new_acc = []
        for h in range(ATT_HEADS):
            a = alpha * acc[h]
            for j in range(grp):
                vh = vbuf[slot, j, pl.ds(h, page, stride=ATT_HEADS), :].astype(BF16)
                a = a + _dot(pb[:, j * page:(j + 1) * page], vh)
            new_acc.append(a)
        return m_new, l_new, tuple(new_acc)

    m, l, acc = lax.fori_loop(0, n_groups, body, (m0, l0, acc0))
    lam = _lam_from_params(lp_ref[...], lam_init)
    sw = sw_ref[...]
    for h in range(ATT_HEADS):
        r0, r1 = 2 * h, 2 * h + 1
        o = acc[h][r0:r0 + 1, :] / l[r0:r0 + 1, :] - lam * (acc[h][r1:r1 + 1, :] / l[r1:r1 + 1, :])
        o_ref[:, h * ATT_V_DIM:(h + 1) * ATT_V_DIM] = _subln(o, sw, lam_init).astype(o_ref.dtype)


def _attn_sample(lam_params, q_bcast, knew_bcast, v_new, subln_w, cache_kt, cache_vf, page_table, lam_init):
    b = q_bcast.shape[0]
    n_pages = page_table.shape[1]
    page = cache_kt.shape[2]
    grp = min(PAGE_GROUP, n_pages)
    slots = PAGE_SLOTS
    assert n_pages % grp == 0 and b * (n_pages // grp) >= slots - 1
    w = ATT_WIDTH
    bc = pl.BlockSpec((None, w, LANES), lambda bi, pt: (bi, 0, 0))
    tok = pl.BlockSpec((None, 1, w), lambda bi, pt: (bi, 0, 0))
    full = lambda a: pl.BlockSpec(a.shape, lambda bi, pt: (0,) * a.ndim)
    hbm = pl.BlockSpec(memory_space=pl.ANY)
    grid_spec = pltpu.PrefetchScalarGridSpec(
        num_scalar_prefetch=1,
        grid=(b,),
        in_specs=[full(lam_params), bc, bc, tok, full(subln_w), hbm, hbm],
        out_specs=tok,
        scratch_shapes=[pltpu.VMEM((slots, grp, w, page), F32),
                        pltpu.VMEM((slots, grp, page * ATT_HEADS, ATT_V_DIM), F32),
                        pltpu.SemaphoreType.DMA((slots,)), pltpu.SemaphoreType.DMA((slots,))],
    )
    return pl.pallas_call(
        functools.partial(_attn_sample_kernel, lam_init),
        grid_spec=grid_spec,
        out_shape=jax.ShapeDtypeStruct((b, 1, w), BF16),
        compiler_params=pltpu.CompilerParams(dimension_semantics=("arbitrary",),
                                             vmem_limit_bytes=VMEM_LIMIT),
        name="attn_sample",
    )(page_table, lam_params, q_bcast, knew_bcast, v_new, subln_w, cache_kt, cache_vf)


def _mix_kernel(x_ref, orec_ref, oatt_ref, gate_ref, wur_ref, wua_ref, wo_ref, ln2_ref, wr_ref, br_ref,
                hp_ref, hn_ref, route_ref, hist_ref):
    y_rec = _dot(orec_ref[...], wur_ref[...])
    y_att = _dot(oatt_ref[...], wua_ref[...])
    g_rec = gate_ref[:, :D_MODEL].astype(F32)
    g_att = gate_ref[:, D_MODEL:].astype(F32)
    mix = (g_rec * y_rec + g_att * y_att).astype(BF16)
    hp = x_ref[...] + _dot(mix, wo_ref[...])
    hp_ref[...] = hp
    ms = jnp.mean(hp * hp, axis=-1, keepdims=True)
    hn = (hp * lax.rsqrt(ms + EPS) * ln2_ref[...]).astype(BF16)
    hn_ref[...] = hn
    logits = _dot(hn, wr_ref[...]) + br_ref[...]
    lt = logits.T[0:N_EXPERTS, :]
    row = lax.broadcasted_iota(jnp.int32, lt.shape, 0).astype(F32)
    vals, idxs = [], []
    for _ in range(TOP_K):
        mk = jnp.max(lt, axis=0, keepdims=True)
        ik = jnp.min(jnp.where(lt == mk, row, float(N_EXPERTS)), axis=0, keepdims=True)
        vals.append(mk)
        idxs.append(ik)
        lt = jnp.where(row == ik, -jnp.inf, lt)
    es = [jnp.exp(v - vals[0]) for v in vals]
    denom = es[0] + es[1] + es[2] + es[3]
    route_ref[...] = jnp.concatenate(idxs + [e / denom for e in es], axis=0)
    hits = jnp.where(row == idxs[0], 1.0, 0.0)
    for kk in range(1, TOP_K):
        hits = hits + jnp.where(row == idxs[kk], 1.0, 0.0)
    hist_ref[...] = jnp.sum(hits, axis=1, keepdims=True)


def _mix(x2d, orec, oatt, gates, wur, wua, wo, ln2, wr, br):
    t = x2d.shape[0]
    tm = min(TOKEN_TILE, t)
    assert t % tm == 0
    row = lambda w: pl.BlockSpec((tm, w), lambda i: (i, 0))
    full = lambda a: pl.BlockSpec(a.shape, lambda i: (0,) * a.ndim)
    return pl.pallas_call(
        _mix_kernel,
        grid=(t // tm,),
        in_specs=[row(D_MODEL), row(REC_WIDTH), row(ATT_WIDTH), row(2 * D_MODEL),
                  full(wur), full(wua), full(wo), full(ln2), full(wr), full(br)],
        out_specs=(row(D_MODEL), row(D_MODEL),
                   pl.BlockSpec((2 * TOP_K, tm), lambda i: (0, i)),
                   pl.BlockSpec((None, N_EXPERTS, 1), lambda i: (i, 0, 0))),
        out_shape=(jax.ShapeDtypeStruct((t, D_MODEL), F32),
                   jax.ShapeDtypeStruct((t, D_MODEL), BF16),
                   jax.ShapeDtypeStruct((2 * TOP_K, t), F32),
                   jax.ShapeDtypeStruct((t // tm, N_EXPERTS, 1), F32)),
        compiler_params=pltpu.CompilerParams(dimension_semantics=("arbitrary",),
                                             vmem_limit_bytes=VMEM_LIMIT),
        name="mix_route",
    )(x2d, orec, oatt, gates, wur, wua, wo, ln2, wr, br)


def _pair_split_matrix():
    m = np.zeros((MXU_DIM, MXU_DIM), np.float32)
    half = MXU_DIM // 2
    m[2 * np.arange(half), np.arange(half)] = 1.0
    m[2 * np.arange(half) + 1, half + np.arange(half)] = 1.0
    return m


def _wprep_kernel(w_ref, p_ref, o_ref):
    for g in range(w_ref.shape[1] // MXU_DIM):
        cols = slice(g * MXU_DIM, (g + 1) * MXU_DIM)
        o_ref[:, cols] = _dot(w_ref[:, cols].astype(BF16), p_ref[...]).astype(o_ref.dtype)


def _prep_up_weights(w_up, split_mat):
    e, d, f2 = w_up.shape
    return pl.pallas_call(
        _wprep_kernel,
        grid=(e,),
        in_specs=[pl.BlockSpec((None, d, f2), lambda i: (i, 0, 0)),
                  pl.BlockSpec(split_mat.shape, lambda i: (0, 0))],
        out_specs=pl.BlockSpec((None, d, f2), lambda i: (i, 0, 0)),
        out_shape=jax.ShapeDtypeStruct((e, d, f2), BF16),
        compiler_params=pltpu.CompilerParams(dimension_semantics=("arbitrary",),
                                             vmem_limit_bytes=VMEM_LIMIT),
        name="moe_wprep",
    )(w_up, split_mat)


def _moe_kernel(wt_ref, we_ref, lo_ref, hi_ref, first_ref, x_ref, wu_ref, bu_ref, wd_ref, bd_ref, rw_ref,
                y_ref):
    del we_ref
    w = pl.program_id(0)
    lo = lo_ref[w]
    hi = hi_ref[w]

    tm = y_ref.shape[0]

    @pl.when(hi > lo)
    def _():
        rw_col = jnp.broadcast_to(rw_ref[...], (LANES, tm)).T[:, 0:1]
        row = lax.broadcasted_iota(jnp.int32, (tm, 1), 0) + wt_ref[w] * tm
        scale = jnp.where((row >= lo) & (row < hi), rw_col, 0.0)
        half = MXU_DIM // 2
        n_grp = 2 * D_FF // MXU_DIM
        h = _dot(x_ref[...], wu_ref[...]) + bu_ref[...]
        glu = jnp.concatenate([h[:, g * MXU_DIM:g * MXU_DIM + half] for g in range(n_grp)], axis=1)
        lin = jnp.concatenate([h[:, g * MXU_DIM + half:(g + 1) * MXU_DIM] for g in range(n_grp)], axis=1)
        glu = jnp.minimum(glu, SWIGLU_LIMIT)
        lin = jnp.clip(lin, -SWIGLU_LIMIT, SWIGLU_LIMIT)
        a = glu * _sigmoid(SWIGLU_ALPHA * glu) * (lin + 1.0)
        y = (scale * (_dot(a.astype(BF16), wd_ref[...]) + bd_ref[...])).astype(y_ref.dtype)

        @pl.when(first_ref[w] == 1)
        def _():
            y_ref[...] = y

        @pl.when(first_ref[w] == 0)
        def _():
            y_ref[...] = y_ref[...] + y


def _moe_ffn(work, x_sorted, wu, bu, wd, bd, row_w, tm):
    n_rows = x_sorted.shape[0]
    assert n_rows % tm == 0
    row = lambda c: pl.BlockSpec((tm, c), lambda w, wt, we, lo, hi, fi: (wt[w], 0))
    per_e = lambda r, c: pl.BlockSpec((None, r, c), lambda w, wt, we, lo, hi, fi: (we[w], 0, 0))
    grid_spec = pltpu.PrefetchScalarGridSpec(
        num_scalar_prefetch=5,
        grid=(work[0].shape[0],),
        in_specs=[row(D_MODEL), per_e(D_MODEL, 2 * D_FF), per_e(1, 2 * D_FF),
                  per_e(D_FF, D_MODEL), per_e(1, D_MODEL),
                  pl.BlockSpec((None, 1, tm), lambda w, wt, we, lo, hi, fi: (wt[w], 0, 0))],
        out_specs=row(D_MODEL),
    )
    return pl.pallas_call(
        _moe_kernel,
        grid_spec=grid_spec,
        out_shape=jax.ShapeDtypeStruct((n_rows, D_MODEL), BF16),
        compiler_params=pltpu.CompilerParams(dimension_semantics=("arbitrary",),
                                             vmem_limit_bytes=VMEM_LIMIT),
        name="moe_ffn",
    )(*work, x_sorted, wu, bu, wd, bd, row_w)


def _combine_kernel(hp_ref, y_ref, o_ref):
    moe = y_ref[0].astype(F32)
    for kk in range(1, TOP_K):
        moe = moe + y_ref[kk].astype(F32)
    o_ref[...] = hp_ref[...] + moe


def _combine(hp, y_rows):
    t = hp.shape[0]
    tm = min(TOKEN_TILE, t)
    assert t % tm == 0
    return pl.pallas_call(
        _combine_kernel,
        grid=(t // tm,),
        in_specs=[pl.BlockSpec((tm, D_MODEL), lambda i: (i, 0)),
                  pl.BlockSpec((TOP_K, tm, D_MODEL), lambda i: (0, i, 0))],
        out_specs=pl.BlockSpec((tm, D_MODEL), lambda i: (i, 0)),
        out_shape=jax.ShapeDtypeStruct((t, D_MODEL), F32),
        compiler_params=pltpu.CompilerParams(dimension_semantics=("arbitrary",),
                                             vmem_limit_bytes=VMEM_LIMIT),
        name="moe_combine",
    )(hp, y_rows)


def _moe(hn, route, hist, wu, bu, wd, bd, run_after=None):
    t = hn.shape[0]
    n = t * TOP_K
    tm = min(MOE_TILE, t)
    assert n % tm == 0
    n_tiles = n // tm
    shift = max(1, (n - 1).bit_length())
    assert N_EXPERTS << shift < 2 ** 31
    e_flat = route[:TOP_K].astype(jnp.int32).reshape(n)
    w_flat = route[TOP_K:2 * TOP_K].reshape(n)
    ar = jnp.arange(n, dtype=jnp.int32)
    keys = lax.sort((e_flat << shift) | ar, is_stable=False)
    order = keys & ((1 << shift) - 1)
    _, pos = lax.sort((order, ar), num_keys=1, is_stable=False)
    counts = jnp.sum(hist, axis=(0, 2)).astype(jnp.int32)
    c_end = jnp.cumsum(counts)
    c_start = c_end - counts
    cuts = lax.sort(jnp.concatenate([jnp.arange(n_tiles, dtype=jnp.int32) * tm, c_start[1:]]), is_stable=False)
    lo = cuts
    hi = jnp.concatenate([cuts[1:], jnp.full((1,), n, jnp.int32)])
    w_tile = jnp.minimum(lo // tm, n_tiles - 1)
    w_expert = jnp.minimum(jnp.sum((c_end[None, :] <= lo[:, None]).astype(jnp.int32), axis=1), N_EXPERTS - 1)
    first = ((lo == w_tile * tm) & (hi > lo)).astype(jnp.int32)
    token = (order & (t - 1)) if t & (t - 1) == 0 else order % t
    x_sorted = hn.at[token].get(mode="promise_in_bounds")
    row_w = w_flat.at[order].get(mode="promise_in_bounds").reshape(n_tiles, 1, tm)
    if run_after is not None:
        row_w, _ = lax.optimization_barrier((row_w, run_after))
    y_sorted = _moe_ffn((w_tile, w_expert, lo, hi, first), x_sorted, wu, bu, wd, bd, row_w, tm)
    return y_sorted.at[pos].get(mode="promise_in_bounds").reshape(TOP_K, t, D_MODEL)


def _seg_mean_matrix():
    g = np.arange(ATT_WIDTH) // ATT_HEAD_DIM
    return jnp.asarray((g[:, None] == g[None, :]).astype(np.float32) / ATT_HEAD_DIM, dtype=BF16)


def kernel(x_prompt, x_sample, cache_k, cache_v, state_rec, page_table, ln1_w, w_in, rec_lb_logits,
           rec_norm_w, w_up_rec, q_norm_w, k_norm_w, lambda_q1, lambda_k1, lambda_q2, lambda_k2,
           att_subln_w, w_up_att, w_out, ln2_w, w_router, b_router, w_exp_up, b_exp_up,
           w_exp_down, b_exp_down):
    depth = ln1_w.shape[0]
    assert depth == 1 and x_sample.shape[1] == 1
    layer = 0
    bp, lp_, d = x_prompt.shape
    bs = x_sample.shape[0]
    tp = bp * lp_
    lam_init = _lambda_init(layer)

    w_in_b = w_in[layer].astype(BF16)
    ln1 = ln1_w[layer].reshape(1, d)
    ln2 = ln2_w[layer].reshape(1, d)
    qn_t = jnp.tile(q_norm_w[layer], ATT_WIDTH // ATT_HEAD_DIM).reshape(1, ATT_WIDTH)
    kn_t = jnp.tile(k_norm_w[layer], ATT_WIDTH // ATT_HEAD_DIM).reshape(1, ATT_WIDTH)
    nw_t = jnp.tile(rec_norm_w[layer], REC_HEADS).reshape(1, REC_WIDTH)
    subln = att_subln_w[layer].reshape(1, ATT_V_DIM)
    lam_params = jnp.stack([lambda_q1[layer], lambda_k1[layer], lambda_q2[layer], lambda_k2[layer]]).astype(F32)
    lb_logits = rec_lb_logits.astype(F32)[layer:layer + 2]
    wur = w_up_rec[layer].astype(BF16)
    wua = w_up_att[layer].astype(BF16)
    wo = w_out[layer].astype(BF16)
    wr = jnp.zeros((d, LANES), BF16).at[:, :N_EXPERTS].set(w_router[layer].astype(BF16))
    br = jnp.full((1, LANES), -1e30, F32).at[0, :N_EXPERTS].set(b_router[layer].astype(F32))
    wu = _prep_up_weights(w_exp_up[layer], jnp.asarray(_pair_split_matrix(), dtype=BF16))
    half = MXU_DIM // 2
    bu = (b_exp_up[layer].reshape(N_EXPERTS, 2 * D_FF // MXU_DIM, half, 2)
          .transpose(0, 1, 3, 2).reshape(N_EXPERTS, 1, 2 * D_FF))
    wd = w_exp_down[layer].astype(BF16)
    bd = b_exp_down[layer].reshape(N_EXPERTS, 1, D_MODEL)
    seg_mean = _seg_mean_matrix()

    xp2 = x_prompt.reshape(tp, d)
    rq, zf, rv, rg, aq, akt, av, gates = _in_proj(xp2, ln1, w_in_b, seg_mean, qn_t, kn_t, seq_len=lp_)
    seq = lambda a: a.reshape(bp, lp_, a.shape[-1])
    o_rec, s_p = _gla_prompt(seq(rq), seq(zf), seq(rv), seq(rg), lb_logits, nw_t)
    o_att = _attn_prompt(lam_params, seq(aq), akt, av.reshape(bp, lp_ * ATT_HEADS, ATT_V_DIM), subln, lam_init)
    hp, hn_p, route_p, hist_p = _mix(xp2, o_rec.reshape(tp, REC_WIDTH), o_att.reshape(tp, ATT_WIDTH), gates,
                                     wur, wua, wo, ln2, wr, br)

    xs2 = x_sample.reshape(bs, d)
    rq_s, zf_s, rv_s, rg_s, aq_s, ak_s, av_s, gates_s = _in_proj(xs2, ln1, w_in_b, seg_mean, qn_t, kn_t)
    tok = lambda a: a.reshape(bs, 1, a.shape[-1])
    o_rec_s, s_s = _rec_step(tok(rq_s), tok(zf_s), tok(rv_s), tok(rg_s), state_rec[layer].astype(F32),
                             lb_logits, nw_t)
    n_pool, page = cache_k.shape[1], cache_k.shape[2]
    ckt = jnp.transpose(cache_k[layer].reshape(n_pool, page, ATT_WIDTH), (0, 2, 1))
    cvf = cache_v[layer].reshape(n_pool, page * ATT_HEADS, ATT_V_DIM)
    q_bcast = jnp.broadcast_to(aq_s.astype(F32)[:, :, None], (bs, ATT_WIDTH, LANES))
    kn_bcast = jnp.broadcast_to(ak_s[:, :, None], (bs, ATT_WIDTH, LANES))
    o_att_s = _attn_sample(lam_params, q_bcast, kn_bcast, av_s.reshape(bs, 1, ATT_WIDTH), subln, ckt, cvf,
                           page_table.astype(jnp.int32), lam_init)
    hs, hn_s, route_s, hist_s = _mix(xs2, o_rec_s.reshape(bs, REC_WIDTH), o_att_s.reshape(bs, ATT_WIDTH),
                                     gates_s, wur, wua, wo, ln2, wr, br)

    y_prompt = _combine(hp, _moe(hn_p, route_p, hist_p, wu, bu, wd, bd,
                                 run_after=(s_s, q_bcast, kn_bcast))).reshape(bp, lp_, d)
    y_sample = _combine(hs, _moe(hn_s, route_s, hist_s, wu, bu, wd, bd)).reshape(bs, 1, d)

    new_k_prompt = jnp.transpose(akt.reshape(bp, ATT_HEADS, 2, ATT_HEAD_DIM, lp_), (0, 4, 1, 2, 3))[None]
    new_v_prompt = av.reshape(1, bp, lp_, ATT_HEADS, ATT_V_DIM)
    new_rec_prompt = s_p.reshape(1, bp, REC_HEADS, REC_HEAD_DIM, REC_HEAD_DIM).astype(state_rec.dtype)
    new_k_sample = ak_s.reshape(1, bs, 1, ATT_HEADS, 2, ATT_HEAD_DIM)
    new_v_sample = av_s.reshape(1, bs, 1, ATT_HEADS, ATT_V_DIM)
    new_rec_sample = s_s.reshape(1, bs, REC_HEADS, REC_HEAD_DIM, REC_HEAD_DIM).astype(state_rec.dtype)
    return (y_prompt, y_sample, new_k_prompt, new_v_prompt, new_rec_prompt,
            new_k_sample, new_v_sample, new_rec_sample)
```

```python
import functools
import math

import numpy as np
import jax
import jax.numpy as jnp
from jax import lax
from jax.experimental import pallas as pl
from jax.experimental.pallas import tpu as pltpu

F32 = jnp.float32
BF16 = jnp.bfloat16

D_MODEL = 1024
REC_WIDTH = 512
REC_HEAD_DIM = 128
REC_HEADS = 4
ATT_WIDTH = 512
ATT_HEAD_DIM = 64
ATT_V_DIM = 128
ATT_HEADS = 4
N_EXPERTS = 32
TOP_K = 4
D_FF = 1024
SWIGLU_LIMIT = 7.0
SWIGLU_ALPHA = 1.702
CHUNK = 64
EPS = 1e-6
N_IN = 4 * REC_WIDTH + 3 * ATT_WIDTH + 2 * D_MODEL
LANES = 128
SUBLANES = 8
MXU_DIM = 256
VMEM_LIMIT = 48 * 1024 * 1024
LOG2E = math.log2(math.e)
Q_SCALE = (ATT_HEAD_DIM ** -0.5) * LOG2E

TOKEN_TILE = 512
GLA_BLOCK = 512
PAGE_SLOTS = 3
ATT_TILE = 512
ATT_ROWS = 256
MOE_TILE = 512
PAGE_GROUP = 16


def _sigmoid(x):
    return 1.0 / (1.0 + jnp.exp(-x))


def _dot(a, b):
    return jnp.dot(a, b, preferred_element_type=F32)


def _dot_nt(a, b):
    return lax.dot_general(a, b, (((1,), (1,)), ((), ())), preferred_element_type=F32)


def _dot_tn(a, b):
    return lax.dot_general(a, b, (((0,), (0,)), ((), ())), preferred_element_type=F32)


def _split_dot(x, m_bf16, terms):
    acc = None
    r = x
    for t in range(terms):
        p = r.astype(BF16)
        d = _dot(p, m_bf16)
        acc = d if acc is None else acc + d
        if t + 1 < terms:
            r = r - p.astype(F32)
    return acc


def _lambda_init(layer):
    return 0.8 - 0.6 * math.exp(-0.3 * layer)


def _lam_from_params(lp, lam_init):
    a = jnp.sum(lp[0:1, :] * lp[1:2, :], axis=-1, keepdims=True)
    b = jnp.sum(lp[2:3, :] * lp[3:4, :], axis=-1, keepdims=True)
    return jnp.exp(a) - jnp.exp(b) + lam_init


def _lower_bound(lbl):
    a0 = lbl[0:1, :]
    a1 = lbl[1:2, :]
    mx = jnp.maximum(a0, a1)
    e0 = jnp.exp(a0 - mx)
    e1 = jnp.exp(a1 - mx)
    return e0 / (e0 + e1)


def _in_proj_kernel(k_feature_major, x_ref, ln_ref, w_ref, seg_ref, qn_ref, kn_ref,
                    rq_ref, zf_ref, rv_ref, rg_ref, aq_ref, ak_ref, av_ref, gate_ref):
    x = x_ref[...]
    ms = jnp.mean(x * x, axis=-1, keepdims=True)
    xn = (x * lax.rsqrt(ms + EPS) * ln_ref[...]).astype(BF16)

    def proj(lo, hi):
        return _dot(xn, w_ref[:, lo:hi])

    def seg_norm(a, w):
        ms_seg = _split_dot(a * a, seg_ref[...], 1)
        return a * lax.rsqrt(ms_seg + EPS) * w

    r = REC_WIDTH
    a0 = 4 * r
    g0 = a0 + 3 * ATT_WIDTH
    gate_ref[...] = _sigmoid(proj(g0, g0 + 2 * D_MODEL)).astype(BF16)
    aq = seg_norm(proj(a0, a0 + ATT_WIDTH), qn_ref[...])
    aq_ref[...] = (aq * Q_SCALE).astype(BF16)
    ak = seg_norm(proj(a0 + ATT_WIDTH, a0 + 2 * ATT_WIDTH), kn_ref[...])
    ak_ref[...] = ak.T if k_feature_major else ak
    av = proj(a0 + 2 * ATT_WIDTH, a0 + 3 * ATT_WIDTH)
    rows = av.shape[0]
    for h in range(ATT_HEADS):
        av_ref[pl.ds(h, rows, stride=ATT_HEADS), :] = av[:, h * ATT_V_DIM:(h + 1) * ATT_V_DIM]
    zf_ref[...] = proj(r, 2 * r)
    rq_ref[...] = proj(0, r).astype(BF16)
    rv_ref[...] = proj(2 * r, 3 * r).astype(BF16)
    rg_ref[...] = proj(3 * r, 4 * r).astype(BF16)


def _in_proj(x2d, ln_w, w_in_bf16, seg_mat, qn_t, kn_t, seq_len=None):
    t = x2d.shape[0]
    tm = min(TOKEN_TILE, t)
    assert t % tm == 0
    row = lambda w: pl.BlockSpec((tm, w), lambda i: (i, 0))
    full = lambda a: pl.BlockSpec(a.shape, lambda i: (0,) * a.ndim)
    if seq_len is None:
        k_shape = jax.ShapeDtypeStruct((t, ATT_WIDTH), F32)
        k_spec = row(ATT_WIDTH)
    else:
        assert seq_len % tm == 0 and t % seq_len == 0
        per_seq = seq_len // tm
        k_shape = jax.ShapeDtypeStruct((t // seq_len, ATT_WIDTH, seq_len), F32)
        k_spec = pl.BlockSpec((None, ATT_WIDTH, tm), lambda i: (i // per_seq, 0, i % per_seq))
    out_shapes = (
        jax.ShapeDtypeStruct((t, REC_WIDTH), BF16),
        jax.ShapeDtypeStruct((t, REC_WIDTH), F32),
        jax.ShapeDtypeStruct((t, REC_WIDTH), BF16),
        jax.ShapeDtypeStruct((t, REC_WIDTH), BF16),
        jax.ShapeDtypeStruct((t, ATT_WIDTH), BF16),
        k_shape,
        jax.ShapeDtypeStruct((t * ATT_HEADS, ATT_V_DIM), F32),
        jax.ShapeDtypeStruct((t, 2 * D_MODEL), BF16),
    )
    v_spec = pl.BlockSpec((tm * ATT_HEADS, ATT_V_DIM), lambda i: (i, 0))
    out_specs = tuple(k_spec if s is k_shape else row(s.shape[1]) for s in out_shapes)
    out_specs = out_specs[:6] + (v_spec,) + out_specs[7:]
    return pl.pallas_call(
        functools.partial(_in_proj_kernel, seq_len is not None),
        grid=(t // tm,),
        in_specs=[row(D_MODEL), full(ln_w), full(w_in_bf16), full(seg_mat), full(qn_t), full(kn_t)],
        out_specs=out_specs,
        out_shape=out_shapes,
        compiler_params=pltpu.CompilerParams(dimension_semantics=("arbitrary",),
                                             vmem_limit_bytes=VMEM_LIMIT),
        name="in_proj",
    )(x2d, ln_w, w_in_bf16, seg_mat, qn_t, kn_t)


def _rec_out(o, nw, g):
    ms = jnp.mean(o * o, axis=-1, keepdims=True)
    return o * lax.rsqrt(ms + EPS) * nw * (g * _sigmoid(g))


def _cumsum_rows(x):
    n = x.shape[0]
    row = lax.broadcasted_iota(jnp.int32, x.shape, 0)
    shift = 1
    while shift < n:
        x = x + jnp.where(row >= shift, pltpu.roll(x, shift, axis=0), 0.0)
        shift *= 2
    return x


def _gla_kernel(q_ref, zf_ref, v_ref, g_ref, lbl_ref, nw_ref, o_ref, sfin_ref, st_ref):
    j = pl.program_id(1)

    @pl.when(j == 0)
    def _():
        st_ref[...] = jnp.zeros_like(st_ref)

    lb = _lower_bound(lbl_ref[...])
    c = CHUNK
    ri = lax.broadcasted_iota(jnp.int32, (c, c), 0)
    ci = lax.broadcasted_iota(jnp.int32, (c, c), 1)
    causal = ri >= ci
    nw = nw_ref[...]
    n_chunks = q_ref.shape[0] // c
    for n in range(n_chunks):
        rows = slice(n * c, (n + 1) * c)
        zf = zf_ref[rows, :]
        log_f = jnp.log(lb + (1.0 - lb) * _sigmoid(zf))
        k = (1.0 - lb) * _sigmoid(-zf)
        b = _cumsum_rows(log_f)
        b_last = b[c - 1:c, :]
        e_pos = jnp.exp(b)
        e_neg = jnp.exp(-b)
        e_tail = jnp.exp(b_last - b)
        decay = jnp.exp(b_last)
        q_e = (q_ref[rows, :].astype(F32) * e_pos).astype(BF16)
        k_e = (k * e_neg).astype(BF16)
        k_t = (k * e_tail).astype(BF16)
        v = v_ref[rows, :]
        g = g_ref[rows, :].astype(F32)
        for h in range(REC_HEADS):
            ln = slice(h * REC_HEAD_DIM, (h + 1) * REC_HEAD_DIM)
            s_t = st_ref[h]
            sc = jnp.where(causal, _dot_nt(q_e[:, ln], k_e[:, ln]), 0.0)
            o = _dot(sc.astype(BF16), v[:, ln]) + _dot_nt(q_e[:, ln], s_t.astype(BF16))
            st_ref[h] = decay[:, ln] * s_t + _dot_tn(v[:, ln], k_t[:, ln])
            o_ref[rows, ln] = _rec_out(o, nw[:, ln], g[:, ln]).astype(o_ref.dtype)

    @pl.when(j == pl.num_programs(1) - 1)
    def _():
        for h in range(REC_HEADS):
            sfin_ref[h] = st_ref[h].T


def _gla_prompt(rq, zf, rv, rg, lb_logits, nw_t):
    b, l, w = rq.shape
    assert l % CHUNK == 0
    lb_rows = min(GLA_BLOCK, l)
    assert l % lb_rows == 0
    seq = pl.BlockSpec((None, lb_rows, w), lambda i, j: (i, j, 0))
    full = lambda a: pl.BlockSpec(a.shape, lambda i, j: (0,) * a.ndim)
    return pl.pallas_call(
        _gla_kernel,
        grid=(b, l // lb_rows),
        in_specs=[seq, seq, seq, seq, full(lb_logits), full(nw_t)],
        out_specs=(seq, pl.BlockSpec((None, REC_HEADS, REC_HEAD_DIM, REC_HEAD_DIM),
                                     lambda i, j: (i, 0, 0, 0))),
        out_shape=(jax.ShapeDtypeStruct((b, l, w), BF16),
                   jax.ShapeDtypeStruct((b, REC_HEADS, REC_HEAD_DIM, REC_HEAD_DIM), F32)),
        scratch_shapes=[pltpu.VMEM((REC_HEADS, REC_HEAD_DIM, REC_HEAD_DIM), F32)],
        compiler_params=pltpu.CompilerParams(dimension_semantics=("arbitrary", "arbitrary"),
                                             vmem_limit_bytes=VMEM_LIMIT),
        name="gla_prompt",
    )(rq, zf, rv, rg, lb_logits, nw_t)


def _rec_step_kernel(q_ref, zf_ref, v_ref, g_ref, s0_ref, lbl_ref, nw_ref, o_ref, s_ref):
    lb = _lower_bound(lbl_ref[...])
    zf = zf_ref[...]
    f = lb + (1.0 - lb) * _sigmoid(zf)
    k = (1.0 - lb) * _sigmoid(-zf)
    q = q_ref[...].astype(F32)
    v = v_ref[...].astype(F32)
    g = g_ref[...].astype(F32)
    nw = nw_ref[...]
    d = REC_HEAD_DIM
    ri = lax.broadcasted_iota(jnp.int32, (d, d), 0)
    for h in range(REC_HEADS):
        ln = slice(h * d, (h + 1) * d)
        m = jnp.where(ri == 0, f[:, ln], jnp.where(ri == 1, k[:, ln], jnp.where(ri == 2, q[:, ln], 0.0)))
        mt = m.T
        s_new = mt[:, 0:1] * s0_ref[h] + mt[:, 1:2] * v[:, ln]
        s_ref[h] = s_new
        o = jnp.sum(mt[:, 2:3] * s_new, axis=0, keepdims=True)
        o_ref[:, ln] = _rec_out(o, nw[:, ln], g[:, ln]).astype(o_ref.dtype)


def _rec_step(rq, zf, rv, rg, s0, lb_logits, nw_t):
    b = rq.shape[0]
    w = rq.shape[-1]
    tok = pl.BlockSpec((None, 1, w), lambda i: (i, 0, 0))
    st = pl.BlockSpec((None, REC_HEADS, REC_HEAD_DIM, REC_HEAD_DIM), lambda i: (i, 0, 0, 0))
    full = lambda a: pl.BlockSpec(a.shape, lambda i: (0,) * a.ndim)
    return pl.pallas_call(
        _rec_step_kernel,
        grid=(b,),
        in_specs=[tok, tok, tok, tok, st, full(lb_logits), full(nw_t)],
        out_specs=(tok, st),
        out_shape=(jax.ShapeDtypeStruct((b, 1, w), BF16),
                   jax.ShapeDtypeStruct(s0.shape, F32)),
        compiler_params=pltpu.CompilerParams(dimension_semantics=("arbitrary",),
                                             vmem_limit_bytes=VMEM_LIMIT),
        name="rec_step",
    )(rq, zf, rv, rg, s0, lb_logits, nw_t)


def _subln(o, w, lam_init):
    ms = jnp.mean(o * o, axis=-1, keepdims=True)
    return o * lax.rsqrt(ms + EPS) * w * (1.0 - lam_init)


def _lane_block_max(x):
    blocks = [x[:, c * LANES:(c + 1) * LANES] for c in range(x.shape[1] // LANES)]
    return functools.reduce(jnp.maximum, blocks)


def _attn_prompt_kernel(lam_init, lp_ref, q_ref, kt_ref, v_ref, sw_ref, o_ref,
                        qs_ref, kb_ref, vb_ref, acc_ref):
    i = pl.program_id(2)
    tq = q_ref.shape[0]
    seq = kt_ref.shape[1]
    hd = ATT_V_DIM

    @pl.when(i == 0)
    def _():
        kb_ref[...] = kt_ref[...].astype(BF16)
        head = pl.program_id(1)
        vb_ref[:, 0:hd] = v_ref[pl.ds(head, seq, stride=ATT_HEADS), :].astype(BF16)
        vb_ref[:, hd:2 * hd] = jnp.ones((seq, hd), BF16)

    q = q_ref[...]
    lane = lax.broadcasted_iota(jnp.int32, q.shape, 1)
    first = lane < ATT_HEAD_DIM
    zero = jnp.zeros_like(q)
    qs_ref[0:tq, :] = jnp.where(first, q, zero)
    qs_ref[tq:2 * tq, :] = jnp.where(first, zero, q)
    neg = jnp.finfo(F32).min
    rows = min(ATT_ROWS, tq)
    n_chunks = 2 * tq // rows

    ri = lax.broadcasted_iota(jnp.int32, (rows, rows), 0)
    ci = lax.broadcasted_iota(jnp.int32, (rows, rows), 1)
    causal = ri >= ci

    def run(n_off):
        for r in range(n_chunks):
            row0 = r * rows
            n_full = n_off + row0 % tq
            qc = qs_ref[row0:row0 + rows, :]
            s_d = jnp.where(causal, _dot(qc, kb_ref[:, n_full:n_full + rows]), neg)
            mx = _lane_block_max(s_d)
            if n_full:
                s_o = _dot(qc, kb_ref[:, 0:n_full])
                mx = jnp.maximum(mx, _lane_block_max(s_o))
            m = jnp.max(mx, axis=-1, keepdims=True)
            acc = _dot(jnp.exp2(s_d - m).astype(BF16), vb_ref[n_full:n_full + rows, :])
            if n_full:
                acc = acc + _dot(jnp.exp2(s_o - m).astype(BF16), vb_ref[0:n_full, :])
            acc_ref[row0:row0 + rows, :] = acc

    for c in range(seq // tq):
        pl.when(i == c)(functools.partial(run, c * tq))

    lam = _lam_from_params(lp_ref[...], lam_init)
    a0 = acc_ref[0:tq, :]
    a1 = acc_ref[tq:2 * tq, :]
    o = a0[:, 0:hd] / a0[:, hd:2 * hd] - lam * (a1[:, 0:hd] / a1[:, hd:2 * hd])
    o_ref[...] = _subln(o, sw_ref[...], lam_init).astype(o_ref.dtype)


def _attn_prompt(lam_params, q, kt, v, subln_w, lam_init):
    b, l, w = q.shape
    tq = min(ATT_TILE, l)
    assert l % tq == 0
    hd = ATT_V_DIM
    qspec = pl.BlockSpec((None, tq, hd), lambda bi, h, i: (bi, i, h))
    ktspec = pl.BlockSpec((None, hd, l), lambda bi, h, i: (bi, h, 0))
    vspec = pl.BlockSpec((None, l * ATT_HEADS, hd), lambda bi, h, i: (bi, 0, 0))
    full = lambda a: pl.BlockSpec(a.shape, lambda bi, h, i: (0,) * a.ndim)
    return pl.pallas_call(
        functools.partial(_attn_prompt_kernel, lam_init),
        grid=(b, ATT_HEADS, l // tq),
        in_specs=[full(lam_params), qspec, ktspec, vspec, full(subln_w)],
        out_specs=qspec,
        out_shape=jax.ShapeDtypeStruct((b, l, w), BF16),
        scratch_shapes=[pltpu.VMEM((2 * tq, hd), BF16), pltpu.VMEM((hd, l), BF16),
                        pltpu.VMEM((l, 2 * hd), BF16), pltpu.VMEM((2 * tq, 2 * hd), F32)],
        compiler_params=pltpu.CompilerParams(
            dimension_semantics=("arbitrary", "arbitrary", "arbitrary"),
            vmem_limit_bytes=VMEM_LIMIT),
        name="attn_prompt",
    )(lam_params, q, kt, v, subln_w)


def _seg_scores(feat_by_tok, q_bcast):
    n = feat_by_tok.shape[1]
    prod = feat_by_tok * q_bcast
    return jnp.sum(prod.reshape(2 * ATT_HEADS, ATT_HEAD_DIM, n), axis=1)


def _attn_sample_kernel(lam_init, pt_ref, lp_ref, qb_ref, knb_ref, vn_ref, sw_ref, ck_ref, cv_ref,
                        o_ref, kbuf, vbuf, sem_k, sem_v):
    b = pl.program_id(0)
    nb = pl.num_programs(0)
    n_pages = pt_ref.shape[1]
    slots = kbuf.shape[0]
    ahead = slots - 1
    grp = kbuf.shape[1]
    n_groups = n_pages // grp
    page = kbuf.shape[3]
    total = nb * n_groups

    def copies(n):
        bi = n // n_groups
        gi = n % n_groups
        slot = n % slots
        out = []
        for j in range(grp):
            pg = pt_ref[bi, gi * grp + j]
            out.append(pltpu.make_async_copy(ck_ref.at[pg], kbuf.at[slot, j], sem_k.at[slot]))
            out.append(pltpu.make_async_copy(cv_ref.at[pg], vbuf.at[slot, j], sem_v.at[slot]))
        return out

    @pl.when(b == 0)
    def _():
        for n in range(ahead):
            for cp in copies(n):
                cp.start()

    qb = qb_ref[...]
    m0 = _seg_scores(knb_ref[...], qb)[:, 0:1]
    l0 = jnp.ones_like(m0)
    acc0 = tuple(jnp.broadcast_to(vn_ref[:, h * ATT_V_DIM:(h + 1) * ATT_V_DIM], (2 * ATT_HEADS, ATT_V_DIM))
                 for h in range(ATT_HEADS))

    def body(gi, carry):
        m, l, acc = carry
        n = b * n_groups + gi
        slot = n % slots

        @pl.when(n + ahead < total)
        def _():
            for cp in copies(n + ahead):
                cp.start()

        for cp in copies(n):
            cp.wait()
        s = jnp.concatenate([_seg_scores(kbuf[slot, j], qb) for j in range(grp)], axis=1)
        m_new = jnp.maximum(m, jnp.max(s, axis=-1, keepdims=True))
        p = jnp.exp2(s - m_new)
        alpha = jnp.exp2(m - m_new)
        l_new = alpha * l + jnp.sum(p, axis=-1, keepdims=True)
        pb = p.astype(BF16)
        new_acc = []
        for h in range(ATT_HEADS):
            a = alpha * acc[h]
            for j in range(grp):
                vh = vbuf[slot, j, pl.ds(h, page, stride=ATT_HEADS), :].astype(BF16)
                a = a + _dot(pb[:, j * page:(j + 1) * page], vh)
            new_acc.append(a)
        return m_new, l_new, tuple(new_acc)

    m, l, acc = lax.fori_loop(0, n_groups, body, (m0, l0, acc0))
    lam = _lam_from_params(lp_ref[...], lam_init)
    sw = sw_ref[...]
    for h in range(ATT_HEADS):
        r0, r1 = 2 * h, 2 * h + 1
        o = acc[h][r0:r0 + 1, :] / l[r0:r0 + 1, :] - lam * (acc[h][r1:r1 + 1, :] / l[r1:r1 + 1, :])
        o_ref[:, h * ATT_V_DIM:(h + 1) * ATT_V_DIM] = _subln(o, sw, lam_init).astype(o_ref.dtype)


def _attn_sample(lam_params, q_bcast, knew_bcast, v_new, subln_w, cache_kt, cache_vf, page_table, lam_init):
    b = q_bcast.shape[0]
    n_pages = page_table.shape[1]
    page = cache_kt.shape[2]
    grp = min(PAGE_GROUP, n_pages)
    slots = PAGE_SLOTS
    assert n_pages % grp == 0 and b * (n_pages // grp) >= slots - 1
    w = ATT_WIDTH
    bc = pl.BlockSpec((None, w, LANES), lambda bi, pt: (bi, 0, 0))
    tok = pl.BlockSpec((None, 1, w), lambda bi, pt: (bi, 0, 0))
    full = lambda a: pl.BlockSpec(a.shape, lambda bi, pt: (0,) * a.ndim)
    hbm = pl.BlockSpec(memory_space=pl.ANY)
    grid_spec = pltpu.PrefetchScalarGridSpec(
        num_scalar_prefetch=1,
        grid=(b,),
        in_specs=[full(lam_params), bc, bc, tok, full(subln_w), hbm, hbm],
        out_specs=tok,
        scratch_shapes=[pltpu.VMEM((slots, grp, w, page), F32),
                        pltpu.VMEM((slots, grp, page * ATT_HEADS, ATT_V_DIM), F32),
                        pltpu.SemaphoreType.DMA((slots,)), pltpu.SemaphoreType.DMA((slots,))],
    )
    return pl.pallas_call(
        functools.partial(_attn_sample_kernel, lam_init),
        grid_spec=grid_spec,
        out_shape=jax.ShapeDtypeStruct((b, 1, w), BF16),
        compiler_params=pltpu.CompilerParams(dimension_semantics=("arbitrary",),
                                             vmem_limit_bytes=VMEM_LIMIT),
        name="attn_sample",
    )(page_table, lam_params, q_bcast, knew_bcast, v_new, subln_w, cache_kt, cache_vf)


def _mix_kernel(x_ref, orec_ref, oatt_ref, gate_ref, wur_ref, wua_ref, wo_ref, ln2_ref, wr_ref, br_ref,
                hp_ref, hn_ref, route_ref, hist_ref):
    y_rec = _dot(orec_ref[...], wur_ref[...])
    y_att = _dot(oatt_ref[...], wua_ref[...])
    g_rec = gate_ref[:, :D_MODEL].astype(F32)
    g_att = gate_ref[:, D_MODEL:].astype(F32)
    mix = (g_rec * y_rec + g_att * y_att).astype(BF16)
    hp = x_ref[...] + _dot(mix, wo_ref[...])
    hp_ref[...] = hp
    ms = jnp.mean(hp * hp, axis=-1, keepdims=True)
    hn = (hp * lax.rsqrt(ms + EPS) * ln2_ref[...]).astype(BF16)
    hn_ref[...] = hn
    logits = _dot(hn, wr_ref[...]) + br_ref[...]
    lt = logits.T[0:N_EXPERTS, :]
    row = lax.broadcasted_iota(jnp.int32, lt.shape, 0).astype(F32)
    vals, idxs = [], []
    for _ in range(TOP_K):
        mk = jnp.max(lt, axis=0, keepdims=True)
        ik = jnp.min(jnp.where(lt == mk, row, float(N_EXPERTS)), axis=0, keepdims=True)
        vals.append(mk)
        idxs.append(ik)
        lt = jnp.where(row == ik, -jnp.inf, lt)
    es = [jnp.exp(v - vals[0]) for v in vals]
    denom = es[0] + es[1] + es[2] + es[3]
    route_ref[...] = jnp.concatenate(idxs + [e / denom for e in es], axis=0)
    hits = jnp.where(row == idxs[0], 1.0, 0.0)
    for kk in range(1, TOP_K):
        hits = hits + jnp.where(row == idxs[kk], 1.0, 0.0)
    hist_ref[...] = jnp.sum(hits, axis=1, keepdims=True)


def _mix(x2d, orec, oatt, gates, wur, wua, wo, ln2, wr, br):
    t = x2d.shape[0]
    tm = min(TOKEN_TILE, t)
    assert t % tm == 0
    row = lambda w: pl.BlockSpec((tm, w), lambda i: (i, 0))
    full = lambda a: pl.BlockSpec(a.shape, lambda i: (0,) * a.ndim)
    return pl.pallas_call(
        _mix_kernel,
        grid=(t // tm,),
        in_specs=[row(D_MODEL), row(REC_WIDTH), row(ATT_WIDTH), row(2 * D_MODEL),
                  full(wur), full(wua), full(wo), full(ln2), full(wr), full(br)],
        out_specs=(row(D_MODEL), row(D_MODEL),
                   pl.BlockSpec((2 * TOP_K, tm), lambda i: (0, i)),
                   pl.BlockSpec((None, N_EXPERTS, 1), lambda i: (i, 0, 0))),
        out_shape=(jax.ShapeDtypeStruct((t, D_MODEL), F32),
                   jax.ShapeDtypeStruct((t, D_MODEL), BF16),
                   jax.ShapeDtypeStruct((2 * TOP_K, t), F32),
                   jax.ShapeDtypeStruct((t // tm, N_EXPERTS, 1), F32)),
        compiler_params=pltpu.CompilerParams(dimension_semantics=("arbitrary",),
                                             vmem_limit_bytes=VMEM_LIMIT),
        name="mix_route",
    )(x2d, orec, oatt, gates, wur, wua, wo, ln2, wr, br)


def _pair_split_matrix():
    m = np.zeros((MXU_DIM, MXU_DIM), np.float32)
    half = MXU_DIM // 2
    m[2 * np.arange(half), np.arange(half)] = 1.0
    m[2 * np.arange(half) + 1, half + np.arange(half)] = 1.0
    return m


def _wprep_kernel(w_ref, p_ref, o_ref):
    for g in range(w_ref.shape[1] // MXU_DIM):
        cols = slice(g * MXU_DIM, (g + 1) * MXU_DIM)
        o_ref[:, cols] = _dot(w_ref[:, cols].astype(BF16), p_ref[...]).astype(o_ref.dtype)


def _prep_up_weights(w_up, split_mat):
    e, d, f2 = w_up.shape
    return pl.pallas_call(
        _wprep_kernel,
        grid=(e,),
        in_specs=[pl.BlockSpec((None, d, f2), lambda i: (i, 0, 0)),
                  pl.BlockSpec(split_mat.shape, lambda i: (0, 0))],
        out_specs=pl.BlockSpec((None, d, f2), lambda i: (i, 0, 0)),
        out_shape=jax.ShapeDtypeStruct((e, d, f2), BF16),
        compiler_params=pltpu.CompilerParams(dimension_semantics=("arbitrary",),
                                             vmem_limit_bytes=VMEM_LIMIT),
        name="moe_wprep",
    )(w_up, split_mat)


def _moe_kernel(wt_ref, we_ref, lo_ref, hi_ref, first_ref, x_ref, wu_ref, bu_ref, wd_ref, bd_ref, rw_ref,
                y_ref):
    del we_ref
    w = pl.program_id(0)
    lo = lo_ref[w]
    hi = hi_ref[w]

    tm = y_ref.shape[0]

    @pl.when(hi > lo)
    def _():
        rw_col = jnp.broadcast_to(rw_ref[...], (LANES, tm)).T[:, 0:1]
        row = lax.broadcasted_iota(jnp.int32, (tm, 1), 0) + wt_ref[w] * tm
        scale = jnp.where((row >= lo) & (row < hi), rw_col, 0.0)
        half = MXU_DIM // 2
        n_grp = 2 * D_FF // MXU_DIM
        h = _dot(x_ref[...], wu_ref[...]) + bu_ref[...]
        glu = jnp.concatenate([h[:, g * MXU_DIM:g * MXU_DIM + half] for g in range(n_grp)], axis=1)
        lin = jnp.concatenate([h[:, g * MXU_DIM + half:(g + 1) * MXU_DIM] for g in range(n_grp)], axis=1)
        glu = jnp.minimum(glu, SWIGLU_LIMIT)
        lin = jnp.clip(lin, -SWIGLU_LIMIT, SWIGLU_LIMIT)
        a = glu * _sigmoid(SWIGLU_ALPHA * glu) * (lin + 1.0)
        y = (scale * (_dot(a.astype(BF16), wd_ref[...]) + bd_ref[...])).astype(y_ref.dtype)

        @pl.when(first_ref[w] == 1)
        def _():
            y_ref[...] = y

        @pl.when(first_ref[w] == 0)
        def _():
            y_ref[...] = y_ref[...] + y


def _moe_ffn(work, x_sorted, wu, bu, wd, bd, row_w, tm):
    n_rows = x_sorted.shape[0]
    assert n_rows % tm == 0
    row = lambda c: pl.BlockSpec((tm, c), lambda w, wt, we, lo, hi, fi: (wt[w], 0))
    per_e = lambda r, c: pl.BlockSpec((None, r, c), lambda w, wt, we, lo, hi, fi: (we[w], 0, 0))
    grid_spec = pltpu.PrefetchScalarGridSpec(
        num_scalar_prefetch=5,
        grid=(work[0].shape[0],),
        in_specs=[row(D_MODEL), per_e(D_MODEL, 2 * D_FF), per_e(1, 2 * D_FF),
                  per_e(D_FF, D_MODEL), per_e(1, D_MODEL),
                  pl.BlockSpec((None, 1, tm), lambda w, wt, we, lo, hi, fi: (wt[w], 0, 0))],
        out_specs=row(D_MODEL),
    )
    return pl.pallas_call(
        _moe_kernel,
        grid_spec=grid_spec,
        out_shape=jax.ShapeDtypeStruct((n_rows, D_MODEL), BF16),
        compiler_params=pltpu.CompilerParams(dimension_semantics=("arbitrary",),
                                             vmem_limit_bytes=VMEM_LIMIT),
        name="moe_ffn",
    )(*work, x_sorted, wu, bu, wd, bd, row_w)


def _combine_kernel(hp_ref, y_ref, o_ref):
    moe = y_ref[0].astype(F32)
    for kk in range(1, TOP_K):
        moe = moe + y_ref[kk].astype(F32)
    o_ref[...] = hp_ref[...] + moe


def _combine(hp, y_rows):
    t = hp.shape[0]
    tm = min(TOKEN_TILE, t)
    assert t % tm == 0
    return pl.pallas_call(
        _combine_kernel,
        grid=(t // tm,),
        in_specs=[pl.BlockSpec((tm, D_MODEL), lambda i: (i, 0)),
                  pl.BlockSpec((TOP_K, tm, D_MODEL), lambda i: (0, i, 0))],
        out_specs=pl.BlockSpec((tm, D_MODEL), lambda i: (i, 0)),
        out_shape=jax.ShapeDtypeStruct((t, D_MODEL), F32),
        compiler_params=pltpu.CompilerParams(dimension_semantics=("arbitrary",),
                                             vmem_limit_bytes=VMEM_LIMIT),
        name="moe_combine",
    )(hp, y_rows)


def _moe(hn, route, hist, wu, bu, wd, bd, run_after=None):
    t = hn.shape[0]
    n = t * TOP_K
    tm = min(MOE_TILE, t)
    assert n % tm == 0
    n_tiles = n // tm
    shift = max(1, (n - 1).bit_length())
    assert N_EXPERTS << shift < 2 ** 31
    e_flat = route[:TOP_K].astype(jnp.int32).reshape(n)
    w_flat = route[TOP_K:2 * TOP_K].reshape(n)
    ar = jnp.arange(n, dtype=jnp.int32)
    keys = lax.sort((e_flat << shift) | ar, is_stable=False)
    order = keys & ((1 << shift) - 1)
    _, pos = lax.sort((order, ar), num_keys=1, is_stable=False)
    counts = jnp.sum(hist, axis=(0, 2)).astype(jnp.int32)
    c_end = jnp.cumsum(counts)
    c_start = c_end - counts
    cuts = lax.sort(jnp.concatenate([jnp.arange(n_tiles, dtype=jnp.int32) * tm, c_start[1:]]), is_stable=False)
    lo = cuts
    hi = jnp.concatenate([cuts[1:], jnp.full((1,), n, jnp.int32)])
    w_tile = jnp.minimum(lo // tm, n_tiles - 1)
    w_expert = jnp.minimum(jnp.sum((c_end[None, :] <= lo[:, None]).astype(jnp.int32), axis=1), N_EXPERTS - 1)
    first = ((lo == w_tile * tm) & (hi > lo)).astype(jnp.int32)
    token = (order & (t - 1)) if t & (t - 1) == 0 else order % t
    x_sorted = hn.at[token].get(mode="promise_in_bounds")
    row_w = w_flat.at[order].get(mode="promise_in_bounds").reshape(n_tiles, 1, tm)
    if run_after is not None:
        row_w, _ = lax.optimization_barrier((row_w, run_after))
    y_sorted = _moe_ffn((w_tile, w_expert, lo, hi, first), x_sorted, wu, bu, wd, bd, row_w, tm)
    return y_sorted.at[pos].get(mode="promise_in_bounds").reshape(TOP_K, t, D_MODEL)


def _seg_mean_matrix():
    g = np.arange(ATT_WIDTH) // ATT_HEAD_DIM
    return jnp.asarray((g[:, None] == g[None, :]).astype(np.float32) / ATT_HEAD_DIM, dtype=BF16)


def kernel(x_prompt, x_sample, cache_k, cache_v, state_rec, page_table, ln1_w, w_in, rec_lb_logits,
           rec_norm_w, w_up_rec, q_norm_w, k_norm_w, lambda_q1, lambda_k1, lambda_q2, lambda_k2,
           att_subln_w, w_up_att, w_out, ln2_w, w_router, b_router, w_exp_up, b_exp_up,
           w_exp_down, b_exp_down):
    depth = ln1_w.shape[0]
    assert depth == 1 and x_sample.shape[1] == 1
    layer = 0
    bp, lp_, d = x_prompt.shape
    bs = x_sample.shape[0]
    tp = bp * lp_
    lam_init = _lambda_init(layer)

    w_in_b = w_in[layer].astype(BF16)
    ln1 = ln1_w[layer].reshape(1, d)
    ln2 = ln2_w[layer].reshape(1, d)
    qn_t = jnp.tile(q_norm_w[layer], ATT_WIDTH // ATT_HEAD_DIM).reshape(1, ATT_WIDTH)
    kn_t = jnp.tile(k_norm_w[layer], ATT_WIDTH // ATT_HEAD_DIM).reshape(1, ATT_WIDTH)
    nw_t = jnp.tile(rec_norm_w[layer], REC_HEADS).reshape(1, REC_WIDTH)
    subln = att_subln_w[layer].reshape(1, ATT_V_DIM)
    lam_params = jnp.stack([lambda_q1[layer], lambda_k1[layer], lambda_q2[layer], lambda_k2[layer]]).astype(F32)
    lb_logits = rec_lb_logits.astype(F32)[layer:layer + 2]
    wur = w_up_rec[layer].astype(BF16)
    wua = w_up_att[layer].astype(BF16)
    wo = w_out[layer].astype(BF16)
    wr = jnp.zeros((d, LANES), BF16).at[:, :N_EXPERTS].set(w_router[layer].astype(BF16))
    br = jnp.full((1, LANES), -1e30, F32).at[0, :N_EXPERTS].set(b_router[layer].astype(F32))
    wu = _prep_up_weights(w_exp_up[layer], jnp.asarray(_pair_split_matrix(), dtype=BF16))
    half = MXU_DIM // 2
    bu = (b_exp_up[layer].reshape(N_EXPERTS, 2 * D_FF // MXU_DIM, half, 2)
          .transpose(0, 1, 3, 2).reshape(N_EXPERTS, 1, 2 * D_FF))
    wd = w_exp_down[layer].astype(BF16)
    bd = b_exp_down[layer].reshape(N_EXPERTS, 1, D_MODEL)
    seg_mean = _seg_mean_matrix()

    xp2 = x_prompt.reshape(tp, d)
    rq, zf, rv, rg, aq, akt, av, gates = _in_proj(xp2, ln1, w_in_b, seg_mean, qn_t, kn_t, seq_len=lp_)
    seq = lambda a: a.reshape(bp, lp_, a.shape[-1])
    o_rec, s_p = _gla_prompt(seq(rq), seq(zf), seq(rv), seq(rg), lb_logits, nw_t)
    o_att = _attn_prompt(lam_params, seq(aq), akt, av.reshape(bp, lp_ * ATT_HEADS, ATT_V_DIM), subln, lam_init)
    hp, hn_p, route_p, hist_p = _mix(xp2, o_rec.reshape(tp, REC_WIDTH), o_att.reshape(tp, ATT_WIDTH), gates,
                                     wur, wua, wo, ln2, wr, br)

    xs2 = x_sample.reshape(bs, d)
    rq_s, zf_s, rv_s, rg_s, aq_s, ak_s, av_s, gates_s = _in_proj(xs2, ln1, w_in_b, seg_mean, qn_t, kn_t)
    tok = lambda a: a.reshape(bs, 1, a.shape[-1])
    o_rec_s, s_s = _rec_step(tok(rq_s), tok(zf_s), tok(rv_s), tok(rg_s), state_rec[layer].astype(F32),
                             lb_logits, nw_t)
    n_pool, page = cache_k.shape[1], cache_k.shape[2]
    ckt = jnp.transpose(cache_k[layer].reshape(n_pool, page, ATT_WIDTH), (0, 2, 1))
    cvf = cache_v[layer].reshape(n_pool, page * ATT_HEADS, ATT_V_DIM)
    q_bcast = jnp.broadcast_to(aq_s.astype(F32)[:, :, None], (bs, ATT_WIDTH, LANES))
    kn_bcast = jnp.broadcast_to(ak_s[:, :, None], (bs, ATT_WIDTH, LANES))
    o_att_s = _attn_sample(lam_params, q_bcast, kn_bcast, av_s.reshape(bs, 1, ATT_WIDTH), subln, ckt, cvf,
                           page_table.astype(jnp.int32), lam_init)
    hs, hn_s, route_s, hist_s = _mix(xs2, o_rec_s.reshape(bs, REC_WIDTH), o_att_s.reshape(bs, ATT_WIDTH),
                                     gates_s, wur, wua, wo, ln2, wr, br)

    y_prompt = _combine(hp, _moe(hn_p, route_p, hist_p, wu, bu, wd, bd,
                                 run_after=(s_s, q_bcast, kn_bcast))).reshape(bp, lp_, d)
    y_sample = _combine(hs, _moe(hn_s, route_s, hist_s, wu, bu, wd, bd)).reshape(bs, 1, d)

    new_k_prompt = jnp.transpose(akt.reshape(bp, ATT_HEADS, 2, ATT_HEAD_DIM, lp_), (0, 4, 1, 2, 3))[None]
    new_v_prompt = av.reshape(1, bp, lp_, ATT_HEADS, ATT_V_DIM)
    new_rec_prompt = s_p.reshape(1, bp, REC_HEADS, REC_HEAD_DIM, REC_HEAD_DIM).astype(state_rec.dtype)
    new_k_sample = ak_s.reshape(1, bs, 1, ATT_HEADS, 2, ATT_HEAD_DIM)
    new_v_sample = av_s.reshape(1, bs, 1, ATT_HEADS, ATT_V_DIM)
    new_rec_sample = s_s.reshape(1, bs, REC_HEADS, REC_HEAD_DIM, REC_HEAD_DIM).astype(state_rec.dtype)
    return (y_prompt, y_sample, new_k_prompt, new_v_prompt, new_rec_prompt,
            new_k_sample, new_v_sample, new_rec_sample)
```
